```python
import jax, jax.numpy as jnp
from jax import lax
import numpy as np

D_MODEL = 2048
BATCH = 16
SEQ = 256
DEPTH = 4
DEC_BATCH = 2
DEC_SEQ = 2048
PAST_LEN = 256

GRID_W = 64
D_MIX = D_MODEL
D_A = D_MIX // 2
D_B = D_MIX - D_A
HGRN_HEAD_DIM = 128
HGRN_HEADS = D_A // HGRN_HEAD_DIM
HGRN_CHUNK = 32
N_A_PROJ = 5
POOL_WINDOWS = (2, 4, 8, 16)
POOL_GROUPS = len(POOL_WINDOWS)
POOL_GROUP_DIM = D_B // POOL_GROUPS
D_IN = N_A_PROJ * D_A + D_B
PEER_HEADS = 8
PEER_N_KEYS = 128
PEER_N_EXPERTS = PEER_N_KEYS * PEER_N_KEYS
PEER_TOPK = 16
PEER_KEY_DIM = 256
PEER_HALF = PEER_KEY_DIM // 2
PEER_BLOCK = 128
N_MOD = 6
EPS = 1e-6

kernel_name = "hymba_hgrn2_pool_peer_diffusion_step"


def rmsnorm(x, w):
    xf = x.astype(jnp.float32)
    y = xf * lax.rsqrt(jnp.mean(xf * xf, axis=-1, keepdims=True) + EPS)
    return (y * w.astype(jnp.float32)).astype(x.dtype)


def adaln(cvec, w_ada_l, b_ada_l):
    mod = jax.nn.silu(cvec) @ w_ada_l + b_ada_l
    return jnp.split(mod[:, None, :], N_MOD, axis=-1)


def gla_chunk_scan(q, k, v, logf, s0):
    b_, h_, t_, kd = q.shape
    vd = v.shape[-1]
    n = t_ // HGRN_CHUNK

    def to_chunks(a):
        return jnp.moveaxis(a.reshape(b_, h_, n, HGRN_CHUNK, a.shape[-1]), 2, 0)

    causal = jnp.tril(jnp.ones((HGRN_CHUNK, HGRN_CHUNK), dtype=bool))
    scale = kd ** -0.5

    def step(s, inp):
        qc, kc, vc, gc = inp
        bcum = jnp.cumsum(gc, axis=-2)
        o_inter = jnp.einsum('bhtk,bhkv->bhtv', qc * jnp.exp(bcum), s)
        diff = bcum[..., :, None, :] - bcum[..., None, :, :]
        decay = jnp.exp(jnp.where(causal[:, :, None], diff, -jnp.inf))
        att = jnp.einsum('bhtk,bhtsk,bhsk->bhts', qc, decay, kc)
        o = (o_inter + jnp.einsum('bhts,bhsv->bhtv', att, vc)) * scale
        blast = bcum[..., -1:, :]
        s_new = jnp.swapaxes(jnp.exp(blast), -1, -2) * s + jnp.einsum(
            'bhsk,bhsv->bhkv', kc * jnp.exp(blast - bcum), vc)
        return s_new, o

    s_fin, out = lax.scan(step, s0.astype(jnp.float32),
                          (to_chunks(q), to_chunks(k), to_chunks(v), to_chunks(logf)))
    out = jnp.moveaxis(out, 0, 2).reshape(b_, h_, t_, vd)
    return out, s_fin


def hgrn2_mixer(p_a, lb_f, lb_b, gnorm_w, s0_f, s0_b):
    b_, t_, _ = p_a.shape
    q, z_f, z_b, i, g = jnp.split(p_a.astype(jnp.float32), N_A_PROJ, axis=-1)

    def heads(a):
        return a.reshape(b_, t_, HGRN_HEADS, HGRN_HEAD_DIM).transpose(0, 2, 1, 3)

    def gates(z, lb):
        f = lb + (1.0 - lb) * jax.nn.sigmoid(z)
        return heads(1.0 - f), heads(jnp.log(f))

    qh, ih = heads(q), heads(i)
    k_f, lf_f = gates(z_f, lb_f)
    k_b, lf_b = gates(z_b, lb_b)
    o_f, s_f = gla_chunk_scan(qh, k_f, ih, lf_f, s0_f)

    def rev(a):
        return jnp.flip(a, axis=2)

    o_b, s_b = gla_chunk_scan(rev(qh), rev(k_b), rev(ih), rev(lf_b), s0_b)
    o = (o_f + rev(o_b)).transpose(0, 2, 1, 3)
    o = o * lax.rsqrt(jnp.mean(o * o, axis=-1, keepdims=True) + EPS) \
        * gnorm_w.astype(jnp.float32).reshape(HGRN_HEADS, HGRN_HEAD_DIM)
    o = o.reshape(b_, t_, D_A) * jax.nn.silu(g)
    return o.astype(p_a.dtype), s_f, s_b


def multiscale_pool(u):
    L = u.shape[-2]
    uf = u.astype(jnp.float32)
    csum = jnp.cumsum(uf, axis=-2)
    csum = jnp.concatenate([jnp.zeros_like(csum[..., :1, :]), csum], axis=-2)
    pos = jnp.arange(L)
    outs = []
    for gi, w in enumerate(POOL_WINDOWS):
        sl = slice(gi * POOL_GROUP_DIM, (gi + 1) * POOL_GROUP_DIM)
        lo = jnp.clip(pos - w // 2, 0, L)
        hi = jnp.clip(pos + w // 2, 0, L)
        cg = csum[..., sl]
        total = jnp.take(cg, hi, axis=-2) - jnp.take(cg, lo, axis=-2)
        count = (hi - lo).astype(jnp.float32)[:, None]
        outs.append(total / count - uf[..., sl])
    return jnp.concatenate(outs, axis=-1).astype(u.dtype)


def pool_mixer(u, w_pool_l, pool_scale_l, grid_rows):
    b_, t_, c_ = u.shape
    if grid_rows is not None:
        y = multiscale_pool(u.reshape(b_, grid_rows, GRID_W, c_)).reshape(b_, t_, c_)
    else:
        y = multiscale_pool(u)
    y = jnp.einsum('btgc,gcd->btgd', y.reshape(b_, t_, POOL_GROUPS, POOL_GROUP_DIM), w_pool_l)
    return y.reshape(b_, t_, c_) * pool_scale_l


def peer(h, w_query_l, sub_keys_l, u_l, v_l):
    shp = h.shape
    xt = h.reshape(-1, PEER_BLOCK, D_MODEL)

    def block(xb):
        q = (xb @ w_query_l).reshape(PEER_BLOCK, PEER_HEADS, 2, PEER_HALF)
        s = jnp.einsum('thpk,hpnk->thpn', q, sub_keys_l).astype(jnp.float32)
        s1, i1 = lax.top_k(s[:, :, 0], PEER_TOPK)
        s2, i2 = lax.top_k(s[:, :, 1], PEER_TOPK)
        cand = (s1[..., :, None] + s2[..., None, :]).reshape(PEER_BLOCK, PEER_HEADS, PEER_TOPK * PEER_TOPK)
        cidx = (i1[..., :, None] * PEER_N_KEYS + i2[..., None, :]).reshape(PEER_BLOCK, PEER_HEADS, PEER_TOPK * PEER_TOPK)
        top_s, pos = lax.top_k(cand, PEER_TOPK)
        idx = jnp.take_along_axis(cidx, pos, axis=-1)
        gate = jax.nn.softmax(top_s, axis=-1)
        act = jax.nn.gelu(jnp.einsum('td,thed->the', xb, u_l[idx]), approximate=False)
        return jnp.einsum('the,thed->td', (gate * act).astype(xb.dtype), v_l[idx])

    return lax.map(block, xt).reshape(shp)


def trunk_layer(x, cvec, s0_f, s0_b, grid_rows, lb_f, lb_b, w_ada_l, b_ada_l, norm_w_l,
                w_in_l, gnorm_w_l, w_pool_l, pool_scale_l, w_out_l, w_query_l, sub_keys_l, u_l, v_l):
    sh1, sc1, g1, sh2, sc2, g2 = adaln(cvec, w_ada_l, b_ada_l)
    h = rmsnorm(x, norm_w_l[0]) * (1.0 + sc1) + sh1
    p = h @ w_in_l
    o_a, s_f, s_b = hgrn2_mixer(p[..., :N_A_PROJ * D_A], lb_f, lb_b, gnorm_w_l, s0_f, s0_b)
    o_b = pool_mixer(p[..., N_A_PROJ * D_A:], w_pool_l, pool_scale_l, grid_rows)
    x = x + g1 * (jnp.concatenate([o_a, o_b], axis=-1) @ w_out_l)
    h = rmsnorm(x, norm_w_l[1]) * (1.0 + sc2) + sh2
    x = x + g2 * peer(h, w_query_l, sub_keys_l, u_l, v_l)
    return x, s_f, s_b


def setup_inputs(seed: int = 0) -> dict:
    key = jax.random.key(seed)
    ks = jax.random.split(key, 20)

    def nrm(k, shape, s):
        return jax.random.normal(k, shape, jnp.float32) * s

    st = (HGRN_HEADS, HGRN_HEAD_DIM, HGRN_HEAD_DIM)
    return {
        "x_prompt": nrm(ks[0], (BATCH, SEQ, D_MODEL), 1.0),
        "x_sample": nrm(ks[1], (DEC_BATCH, DEC_SEQ, D_MODEL), 1.0),
        "state_hgrn": nrm(ks[2], (DEC_BATCH, DEPTH, 2) + st, 0.5),
        "c": nrm(ks[3], (DEC_BATCH, D_MODEL), 1.0),
        "c_ctx": nrm(ks[4], (D_MODEL,), 1.0),
        "w_ada": nrm(ks[5], (DEPTH, D_MODEL, N_MOD * D_MODEL), 0.5 * D_MODEL ** -0.5),
        "b_ada": nrm(ks[6], (DEPTH, N_MOD * D_MODEL), 0.02),
        "norm_w": 1.0 + nrm(ks[7], (DEPTH, 2, D_MODEL), 0.05),
        "w_in": nrm(ks[8], (DEPTH, D_MODEL, D_IN), D_MODEL ** -0.5),
        "lower_bounds": nrm(ks[9], (2, DEPTH, D_A), 0.5),
        "hgrn_norm_w": 1.0 + nrm(ks[10], (DEPTH, D_A), 0.05),
        "w_pool": nrm(ks[11], (DEPTH, POOL_GROUPS, POOL_GROUP_DIM, POOL_GROUP_DIM), POOL_GROUP_DIM ** -0.5),
        "pool_scale": 1.0 + nrm(ks[12], (DEPTH, D_B), 0.05),
        "w_out": nrm(ks[13], (DEPTH, D_MODEL, D_MODEL), D_MODEL ** -0.5),
        "w_query": nrm(ks[14], (DEPTH, D_MODEL, PEER_HEADS * PEER_KEY_DIM), D_MODEL ** -0.5),
        "sub_keys": nrm(ks[15], (DEPTH, PEER_HEADS, 2, PEER_N_KEYS, PEER_HALF), PEER_HALF ** -0.5),
        "expert_u": nrm(ks[16], (DEPTH, PEER_N_EXPERTS, D_MODEL), D_MODEL ** -0.5),
        "expert_v": nrm(ks[17], (DEPTH, PEER_N_EXPERTS, D_MODEL), 0.25),
        "final_norm_w": 1.0 + nrm(ks[18], (D_MODEL,), 0.05),
    }


def reference(x_prompt, x_sample, state_hgrn, c, c_ctx, w_ada, b_ada, norm_w, w_in, lower_bounds,
              hgrn_norm_w, w_pool, pool_scale, w_out, w_query, sub_keys, expert_u, expert_v,
              final_norm_w):
    lb_soft = jax.nn.softmax(lower_bounds.astype(jnp.float32), axis=1)
    lb = jnp.cumsum(lb_soft, axis=1) - lb_soft[:, :1]
    grid_rows = x_sample.shape[1] // GRID_W
    zero_state = jnp.zeros((x_prompt.shape[0], HGRN_HEADS, HGRN_HEAD_DIM, HGRN_HEAD_DIM), jnp.float32)
    xp, xs = x_prompt, x_sample
    ctx_states = []
    for l in range(DEPTH):
        xp, s_f, s_b = trunk_layer(xp, c_ctx[None, :], zero_state, zero_state, None, lb[0, l], lb[1, l],
                                   w_ada[l], b_ada[l], norm_w[l], w_in[l], hgrn_norm_w[l], w_pool[l],
                                   pool_scale[l], w_out[l], w_query[l], sub_keys[l], expert_u[l], expert_v[l])
        ctx_states.append(jnp.stack([s_f, s_b], axis=1))
        xs, _, _ = trunk_layer(xs, c, state_hgrn[:, l, 0], state_hgrn[:, l, 1], grid_rows, lb[0, l], lb[1, l],
                               w_ada[l], b_ada[l], norm_w[l], w_in[l], hgrn_norm_w[l], w_pool[l],
                               pool_scale[l], w_out[l], w_query[l], sub_keys[l], expert_u[l], expert_v[l])
    y_prompt = rmsnorm(xp, final_norm_w)
    y_sample = rmsnorm(xs, final_norm_w)
    state_hgrn_new = jnp.stack(ctx_states, axis=1).astype(x_prompt.dtype)
    return (y_prompt, y_sample, state_hgrn_new)
```

```python
import functools

import numpy as np
import jax
import jax.numpy as jnp
from jax import lax
from jax.experimental import pallas as pl
from jax.experimental.pallas import tpu as pltpu

GRID_W = 64
HEAD_DIM = 128
N_MOD = 6
N_A_PROJ = 5
POOL_WINDOWS = (2, 4, 8, 16)
PEER_TOPK = 16
EPS = 1e-6

HGRN_CHUNK = 64
HGRN_BAND = 8
HGRN_TBLOCK = 256
VMEM_LIMIT_BYTES = 52 * 1024 * 1024

_BF = jnp.bfloat16
_F32 = jnp.float32


def _cparams(sem):
    return pltpu.CompilerParams(dimension_semantics=sem, vmem_limit_bytes=VMEM_LIMIT_BYTES)


def _sigmoid(x):
    return 1.0 / (1.0 + jnp.exp(-x))


def _dot(a, b):
    return jnp.dot(a, b, preferred_element_type=_F32)


def _dot_nt(a, b):
    return lax.dot_general(a, b, (((1,), (1,)), ((), ())), preferred_element_type=_F32)


def _dot_tn(a, b):
    return lax.dot_general(a, b, (((0,), (0,)), ((), ())), preferred_element_type=_F32)


def _split3(x):
    x1 = x.astype(_BF)
    r = x - x1.astype(_F32)
    x2 = r.astype(_BF)
    x3 = (r - x2.astype(_F32)).astype(_BF)
    return x1, x2, x3


def _rmsnorm_rows(x, w):
    return x * lax.rsqrt(jnp.mean(x * x, axis=-1, keepdims=True) + EPS) * w


def _adaln_body(cv_ref, w_ref, b_ref, o_ref):
    cv = cv_ref[...]
    s = (cv * _sigmoid(cv)).astype(_BF)
    o_ref[...] = _dot(s, w_ref[...].astype(_BF)) + b_ref[...]


def adaln_all(cv, w_ada, b_ada):
    depth, d, nm = w_ada.shape
    tn = min(nm, 1536)
    assert nm % tn == 0
    return pl.pallas_call(
        _adaln_body,
        grid=(depth, nm // tn),
        in_specs=[
            pl.BlockSpec((8, d), lambda l, j: (0, 0)),
            pl.BlockSpec((None, d, tn), lambda l, j: (l, 0, j)),
            pl.BlockSpec((None, 1, tn), lambda l, j: (l, 0, j)),
        ],
        out_specs=pl.BlockSpec((None, 8, tn), lambda l, j: (l, 0, j)),
        out_shape=jax.ShapeDtypeStruct((depth, 8, nm), _F32),
        compiler_params=_cparams(("arbitrary", "arbitrary")),
        name="adaln",
    )(cv, w_ada, b_ada.reshape(depth, 1, nm))


def _mod_row(i, tm, n_ctx, dec_seq):
    start = i * tm
    return jnp.where(start < n_ctx, 0, 1 + (start - n_ctx) // dec_seq)


def _inproj_body(x_ref, mod_ref, nw_ref, w_ref, p_ref, h_ref):
    @pl.when(pl.program_id(1) == 0)
    def _():
        y = _rmsnorm_rows(x_ref[...], nw_ref[...])
        h_ref[...] = (y * (1.0 + mod_ref[1:2, :]) + mod_ref[0:1, :]).astype(_BF)

    p_ref[...] = _dot(h_ref[...], w_ref[...])


def in_projection(x, mods_l, nw, w_in_bf, n_ctx, dec_seq):
    ntok, d = x.shape
    d_in = w_in_bf.shape[1]
    tm = 512
    tn = d_in // 4
    assert ntok % tm == 0 and n_ctx % tm == 0 and dec_seq % tm == 0 and tn % 128 == 0
    row = functools.partial(_mod_row, tm=tm, n_ctx=n_ctx, dec_seq=dec_seq)
    return pl.pallas_call(
        _inproj_body,
        grid=(ntok // tm, d_in // tn),
        in_specs=[
            pl.BlockSpec((tm, d), lambda i, j: (i, 0)),
            pl.BlockSpec((None, N_MOD, d), lambda i, j: (row(i), 0, 0)),
            pl.BlockSpec((1, d), lambda i, j: (0, 0)),
            pl.BlockSpec((d, tn), lambda i, j: (0, j)),
        ],
        out_specs=pl.BlockSpec((tm, tn), lambda i, j: (i, j)),
        out_shape=jax.ShapeDtypeStruct((ntok, d_in), _F32),
        scratch_shapes=[pltpu.VMEM((tm, d), _BF)],
        compiler_params=_cparams(("arbitrary", "arbitrary")),
        name="inproj",
    )(x, mods_l, nw, w_in_bf)


def _hgrn_levels(c):
    r, out = HGRN_BAND, []
    while r < c:
        out.append(r)
        r *= 2
    return out


def _hgrn_constants(c):
    levels = _hgrn_levels(c)
    nl = len(levels)
    nr = 2 + nl + (HGRN_BAND - 1)
    a = np.zeros((2, nr * c, c), np.float32)
    lmask = np.zeros((2, nl, c, c), np.float32)
    rowsel = np.zeros((2, nl, c, HEAD_DIM), np.float32)
    bmask = np.zeros((2, HGRN_BAND - 1, c, HEAD_DIM), np.float32)
    for d in range(2):
        def pos(t):
            return t if d == 0 else c - 1 - t
        for t in range(c):
            pt = pos(t)
            for u in range(c):
                pu = pos(u)
                if pu <= pt:
                    a[d, t, u] = 1.0
                if pu > pt:
                    a[d, c + t, u] = 1.0
            for li, r in enumerate(levels):
                sb, half = pt // (2 * r), (pt // r) % 2
                ref = sb * 2 * r + r - 1
                rowsel[d, li, t, :] = float(half)
                for u in range(c):
                    pu = pos(u)
                    if half == 1 and ref < pu <= pt:
                        a[d, (2 + li) * c + t, u] = 1.0
                    if half == 0 and pt < pu <= ref:
                        a[d, (2 + li) * c + t, u] = 1.0
                    if half == 1 and pu // (2 * r) == sb and (pu // r) % 2 == 0:
                        lmask[d, li, t, u] = 1.0
            for dd in range(1, HGRN_BAND):
                if pt % HGRN_BAND >= dd:
                    bmask[d, dd - 1, t, :] = 1.0
                    for u in range(c):
                        if pt - dd < pos(u) <= pt:
                            a[d, (2 + nl + dd - 1) * c + t, u] = 1.0
    a3 = np.concatenate([a, a, a], axis=2)
    return (jnp.asarray(a3, _BF), jnp.asarray(lmask), jnp.asarray(rowsel), jnp.asarray(bmask))


def _hgrn_body(fwd_ref, bwd_ref, seq_ref, first_ref, last_ref,
               qf_ref, zf_ref, vf_ref, qb_ref, zb_ref, vb_ref, lbf_ref, lbb_ref, s0_ref,
               a_ref, lmask_ref, rowsel_ref, bmask_ref,
               of_ref, ob_ref, sfin_ref,
               st_ref, x_ref, k_ref):
    del fwd_ref, bwd_ref, seq_ref, last_ref
    c = HGRN_CHUNK
    tb = qf_ref.shape[0]
    nchunk = tb // c
    nl = lmask_ref.shape[1]
    scale = HEAD_DIM ** -0.5
    i = pl.program_id(1)

    @pl.when(first_ref[i] == 1)
    def _():
        st_ref[0] = s0_ref[0].T
        st_ref[1] = s0_ref[1].T

    dirs = ((qf_ref, zf_ref, vf_ref, lbf_ref, of_ref), (qb_ref, zb_ref, vb_ref, lbb_ref, ob_ref))
    for ci in range(nchunk):
        for d in range(2):
            q_ref, z_ref, v_ref, lb_ref, o_ref = dirs[d]
            r0 = (ci if d == 0 else nchunk - 1 - ci) * c
            rows = slice(r0, r0 + c)
            lb = lb_ref[...]
            f = lb + (1.0 - lb) * _sigmoid(z_ref[rows, :])
            k_ref[...] = 1.0 - f
            g1, g2, g3 = _split3(jnp.log(f))
            x_ref[...] = _dot(a_ref[d], jnp.concatenate([g1, g2, g3], axis=0))

            q = q_ref[rows, :]
            v = v_ref[rows, :]
            k = k_ref[...]
            vb16 = v.astype(_BF)
            st = st_ref[d]
            o = _dot_nt((q * jnp.exp(x_ref[0:c, :])).astype(_BF), st.astype(_BF))
            att = jnp.zeros((c, c), _F32)
            for li in range(nl):
                xl = x_ref[(2 + li) * c:(3 + li) * c, :]
                zl = (jnp.where(rowsel_ref[d, li] > 0.5, q, k) * jnp.exp(xl)).astype(_BF)
                att = att + jnp.where(lmask_ref[d, li] > 0.5, _dot_nt(zl, zl), 0.0)
            o = o + _dot(att.astype(_BF), vb16)
            o = o + jnp.sum(q * k, axis=-1, keepdims=True) * v
            for dd in range(1, HGRN_BAND):
                xd = x_ref[(2 + nl + dd - 1) * c:(2 + nl + dd) * c, :]
                sh = dd if d == 0 else c - dd
                kd = pltpu.roll(k, sh, 0)
                vd = pltpu.roll(v, sh, 0)
                w = jnp.where(bmask_ref[d, dd - 1] > 0.5, q * kd * jnp.exp(xd), 0.0)
                o = o + jnp.sum(w, axis=-1, keepdims=True) * vd
            o_ref[rows, :] = o * scale

            ke = (k * jnp.exp(x_ref[c:2 * c, :])).astype(_BF)
            last_row = c - 1 if d == 0 else 0
            decay = jnp.exp(x_ref[last_row:last_row + 1, :])
            st_ref[d] = st * decay + _dot_tn(vb16, ke)

    sfin_ref[0] = st_ref[0].T
    sfin_ref[1] = st_ref[1].T


def hgrn_scan(p, lb_f, lb_b, s0_all, tables, consts):
    ntok = p.shape[0]
    n_seq, _, heads = s0_all.shape[:3]
    hd = HEAD_DIM
    tb = HGRN_TBLOCK
    n_items = tables[0].shape[0]
    a3, lmask, rowsel, bmask = consts

    def col(base, use_bwd):
        def imap(h, i, fwd, bwd, seq, first, last):
            return ((bwd if use_bwd else fwd)[i], base * heads + h)
        return pl.BlockSpec((tb, hd), imap)

    def full(arr):
        nd = arr.ndim
        return pl.BlockSpec(arr.shape, lambda h, i, *_: (0,) * nd)

    state_spec = pl.BlockSpec((None, 2, None, hd, hd),
                              lambda h, i, fwd, bwd, seq, first, last: (seq[i], 0, h, 0, 0))
    lb_spec = pl.BlockSpec((1, hd), lambda h, i, *_: (0, h))
    grid_spec = pltpu.PrefetchScalarGridSpec(
        num_scalar_prefetch=5,
        grid=(heads, n_items),
        in_specs=[col(0, False), col(1, False), col(3, False),
                  col(0, True), col(2, True), col(3, True),
                  lb_spec, lb_spec, state_spec,
                  full(a3), full(lmask), full(rowsel), full(bmask)],
        out_specs=[
            pl.BlockSpec((tb, hd), lambda h, i, fwd, bwd, seq, first, last: (fwd[i], h)),
            pl.BlockSpec((tb, hd), lambda h, i, fwd, bwd, seq, first, last: (bwd[i], h)),
            state_spec,
        ],
        scratch_shapes=[
            pltpu.VMEM((2, hd, hd), _F32),
            pltpu.VMEM((a3.shape[1], hd), _F32),
            pltpu.VMEM((HGRN_CHUNK, hd), _F32),
        ],
    )
    return pl.pallas_call(
        _hgrn_body,
        grid_spec=grid_spec,
        out_shape=[
            jax.ShapeDtypeStruct((ntok, heads * hd), _F32),
            jax.ShapeDtypeStruct((ntok, heads * hd), _F32),
            jax.ShapeDtypeStruct((n_seq, 2, heads, hd, hd), _F32),
        ],
        compiler_params=_cparams(("arbitrary", "arbitrary")),
        name="hgrn",
    )(*tables, p, p, p, p, p, p, lb_f, lb_b, s0_all, a3, lmask, rowsel, bmask)


def _hgrn_tables(batch, seq, dec_batch, dec_seq):
    tb = HGRN_TBLOCK
    assert seq % tb == 0 and dec_seq % tb == 0
    fwd, bwd, sq, first, last = [], [], [], [], []
    base = 0
    for s_idx, (nb, n) in enumerate([(seq // tb, batch), (dec_seq // tb, dec_batch)]):
        for s in range(n):
            for j in range(nb):
                fwd.append(base + j)
                bwd.append(base + nb - 1 - j)
                sq.append(s if s_idx == 0 else batch + s)
                first.append(int(j == 0))
                last.append(int(j == nb - 1))
            base += nb
    return tuple(jnp.asarray(np.array(t, np.int32)) for t in (fwd, bwd, sq, first, last))


def _pool_constants(tm, seg_lens):
    nw = len(POOL_WINDOWS)
    pw = np.zeros((len(seg_lens), nw, tm, tm), np.float32)
    cnt = np.zeros((len(seg_lens), nw, tm, 1), np.float32)
    for si, seg in enumerate(seg_lens):
        for wi, w in enumerate(POOL_WINDOWS):
            for t in range(tm):
                s0, ps = (t // seg) * seg, t % seg
                lo, hi = max(ps - w // 2, 0), min(ps + w // 2, seg)
                pw[si, wi, t, s0 + lo:s0 + hi] = 1.0
                cnt[si, wi, t, 0] = hi - lo
    pw3 = np.concatenate([pw, pw, pw], axis=3)
    return jnp.asarray(pw3, _BF), cnt


def _mix_body(x_ref, of_ref, ob_ref, g_ref, u_ref, mod_ref, gw_ref, pw_ref, cnt_ref, wpool_ref,
              pscale_ref, wout_ref, nw2_ref, x1_ref, h2_ref):
    d_a = of_ref.shape[1]
    heads = d_a // HEAD_DIM
    ngroups, gd = wpool_ref.shape[0], wpool_ref.shape[1]

    o = of_ref[...] + ob_ref[...]
    g = g_ref[...]
    gate = g * _sigmoid(g)
    pieces = []
    for h in range(heads):
        sl = slice(h * HEAD_DIM, (h + 1) * HEAD_DIM)
        pieces.append((_rmsnorm_rows(o[:, sl], gw_ref[:, sl]) * gate[:, sl]).astype(_BF))

    u = u_ref[...]
    u1, u2, u3 = _split3(u)
    us = jnp.concatenate([u1, u2, u3], axis=0)
    for gi in range(ngroups):
        sl = slice(gi * gd, (gi + 1) * gd)
        y = _dot(pw_ref[gi], us[:, sl]) / cnt_ref[gi] - u[:, sl]
        pieces.append((_dot(y.astype(_BF), wpool_ref[gi]) * pscale_ref[:, sl]).astype(_BF))

    cat = jnp.concatenate(pieces, axis=-1)
    x1 = x_ref[...] + mod_ref[2:3, :] * _dot(cat, wout_ref[...])
    x1_ref[...] = x1
    y2 = _rmsnorm_rows(x1, nw2_ref[...])
    h2_ref[...] = (y2 * (1.0 + mod_ref[4:5, :]) + mod_ref[3:4, :]).astype(_BF)


def mixer_epilogue(x, p, o_f, o_b, mods_l, gnorm_w, pool_consts, w_pool_bf, pool_scale, w_out_bf, nw2,
                   n_ctx, dec_seq):
    ntok, d = x.shape
    d_a = o_f.shape[1]
    d_b = p.shape[1] - N_A_PROJ * d_a
    assert d_a == d_b
    tm = 256
    pw3, cnt = pool_consts
    cnt = jnp.broadcast_to(jnp.asarray(cnt), cnt.shape[:3] + (w_pool_bf.shape[1],))
    row = functools.partial(_mod_row, tm=tm, n_ctx=n_ctx, dec_seq=dec_seq)

    def is_sample(i):
        return jnp.where(i * tm < n_ctx, 0, 1)

    return pl.pallas_call(
        _mix_body,
        grid=(ntok // tm,),
        in_specs=[
            pl.BlockSpec((tm, d), lambda i: (i, 0)),
            pl.BlockSpec((tm, d_a), lambda i: (i, 0)),
            pl.BlockSpec((tm, d_a), lambda i: (i, 0)),
            pl.BlockSpec((tm, d_a), lambda i: (i, 4)),
            pl.BlockSpec((tm, d_b), lambda i: (i, 5)),
            pl.BlockSpec((None, N_MOD, d), lambda i: (row(i), 0, 0)),
            pl.BlockSpec((1, d_a), lambda i: (0, 0)),
            pl.BlockSpec((None,) + pw3.shape[1:], lambda i: (is_sample(i), 0, 0, 0)),
            pl.BlockSpec((None,) + cnt.shape[1:], lambda i: (is_sample(i), 0, 0, 0)),
            pl.BlockSpec(w_pool_bf.shape, lambda i: (0, 0, 0)),
            pl.BlockSpec((1, d_b), lambda i: (0, 0)),
            pl.BlockSpec(w_out_bf.shape, lambda i: (0, 0)),
            pl.BlockSpec((1, d), lambda i: (0, 0)),
        ],
        out_specs=[pl.BlockSpec((tm, d), lambda i: (i, 0)), pl.BlockSpec((tm, d), lambda i: (i, 0))],
        out_shape=[jax.ShapeDtypeStruct((ntok, d), _F32), jax.ShapeDtypeStruct((ntok, d), _BF)],
        compiler_params=_cparams(("arbitrary",)),
        name="mixer",
    )(x, o_f, o_b, p, p, mods_l, gnorm_w, pw3, cnt, w_pool_bf, pool_scale, w_out_bf, nw2)


def _topk_rows(s, iota, k, fill):
    n = s.shape[0]
    vals, ids = [], []
    for _ in range(k):
        m = jnp.max(s, axis=0, keepdims=True)
        i = jnp.min(jnp.where(s == m, iota, n), axis=0, keepdims=True)
        s = jnp.where(iota == i, fill, s)
        vals.append(m)
        ids.append(i)
    return vals, ids


def _route_body(h_ref, wq_ref, keys_ref, gate_ref, ia_ref, ib_ref, q_scr, g_scr, a_scr, b_scr):
    tr = h_ref.shape[0]
    nk, half = keys_ref.shape[1], keys_ref.shape[2]
    heads = keys_ref.shape[0] // 2
    topk = PEER_TOPK
    ncand = topk * topk
    neg = -jnp.inf

    q = _dot(h_ref[...], wq_ref[...]).astype(_BF)
    for hp in range(2 * heads):
        q_scr[hp] = q[:, hp * half:(hp + 1) * half]

    iota_k = lax.broadcasted_iota(jnp.int32, (nk, tr), 0)
    iota_c = lax.broadcasted_iota(jnp.int32, (ncand, tr), 0)

    def head(h, carry):
        tops, idxs = [], []
        for p in range(2):
            s = _dot_nt(keys_ref[2 * h + p], q_scr[2 * h + p])
            vals, ids = _topk_rows(s, iota_k, topk, neg)
            tops.append(vals)
            idxs.append(ids)
        s2 = jnp.concatenate(tops[1], axis=0)
        i2 = jnp.concatenate(idxs[1], axis=0)
        cand = jnp.concatenate([tops[0][a] + s2 for a in range(topk)], axis=0)
        cidx = jnp.concatenate([idxs[0][a] * nk + i2 for a in range(topk)], axis=0)
        sel_s, sel_e = [], []
        for _ in range(topk):
            m = jnp.max(cand, axis=0, keepdims=True)
            pos = jnp.min(jnp.where(cand == m, iota_c, ncand), axis=0, keepdims=True)
            hit = iota_c == pos
            sel_e.append(jnp.max(jnp.where(hit, cidx, -1), axis=0, keepdims=True))
            cand = jnp.where(hit, neg, cand)
            sel_s.append(m)
        top_s = jnp.concatenate(sel_s, axis=0)
        e = jnp.concatenate(sel_e, axis=0)
        ex = jnp.exp(top_s - jnp.max(top_s, axis=0, keepdims=True))
        gate = ex / jnp.sum(ex, axis=0, keepdims=True)
        rows = pl.ds(pl.multiple_of(h * topk, topk), topk)
        g_scr[rows, :] = gate
        a_scr[rows, :] = lax.shift_right_logical(e, nk.bit_length() - 1).astype(_F32)
        b_scr[rows, :] = lax.bitwise_and(e, nk - 1).astype(_F32)
        return carry

    lax.fori_loop(0, heads, head, 0)
    gate_ref[...] = g_scr[...].T
    ia_ref[...] = a_scr[...].T
    ib_ref[...] = b_scr[...].T


def peer_route(h2, w_query_bf, keys_bf):
    ntok, d = h2.shape
    hq = w_query_bf.shape[1]
    nhp, nk, half = keys_bf.shape
    tr = 128
    npair = (nhp // 2) * PEER_TOPK
    assert npair == tr == nk
    out = jax.ShapeDtypeStruct((ntok, npair), _F32)
    spec = pl.BlockSpec((tr, npair), lambda i: (i, 0))
    return pl.pallas_call(
        _route_body,
        grid=(ntok // tr,),
        in_specs=[
            pl.BlockSpec((tr, d), lambda i: (i, 0)),
            pl.BlockSpec((d, hq), lambda i: (0, 0)),
            pl.BlockSpec((nhp, nk, half), lambda i: (0, 0, 0)),
        ],
        out_specs=[spec, spec, spec],
        out_shape=[out, out, out],
        scratch_shapes=[
            pltpu.VMEM((nhp, tr, half), _BF),
            pltpu.VMEM((npair, tr), _F32),
            pltpu.VMEM((npair, tr), _F32),
            pltpu.VMEM((npair, tr), _F32),
        ],
        compiler_params=_cparams(("arbitrary",)),
        name="peer_route",
    )(h2, w_query_bf, keys_bf)


def _wbuild_body(g_ref, a_ref, b_ref, w_ref):
    tw, npair = g_ref.shape
    nk = w_ref.shape[1]
    iota = lax.broadcasted_iota(jnp.int32, (nk, npair), 0).astype(_F32)

    def body(t, carry):
        g = g_ref[pl.ds(t, 1), :]
        g_hi = g.astype(_BF).astype(_F32)
        g_lo = g - g_hi
        hit_a = iota == a_ref[pl.ds(t, 1), :]
        hit_b = iota == b_ref[pl.ds(t, 1), :]
        at = jnp.concatenate([jnp.where(hit_a, g_hi, 0.0), jnp.where(hit_a, g_lo, 0.0)], axis=1).astype(_BF)
        bt1 = jnp.where(hit_b, 1.0, 0.0).astype(_BF)
        w_ref[t] = _dot_nt(at, jnp.concatenate([bt1, bt1], axis=1))
        return carry

    lax.fori_loop(0, tw, body, 0)


def peer_weights(gate, ia, ib, nk):
    ntok, npair = gate.shape
    tw = 128
    spec = pl.BlockSpec((tw, npair), lambda i: (i, 0))
    return pl.pallas_call(
        _wbuild_body,
        grid=(ntok // tw,),
        in_specs=[spec, spec, spec],
        out_specs=pl.BlockSpec((tw, nk, nk), lambda i: (i, 0, 0)),
        out_shape=jax.ShapeDtypeStruct((ntok, nk, nk), _F32),
        compiler_params=_cparams(("arbitrary",)),
        name="peer_weights",
    )(gate, ia, ib)


_SUB = 8


def _ffn_body(h_ref, u_ref, v_ref, w_ref, x_ref, mod_ref, fw_ref, o_ref, acc_ref, *, final):
    j = pl.program_id(1)
    tm = h_ref.shape[0]
    nk = w_ref.shape[2]

    @pl.when(j == 0)
    def _():
        acc_ref[...] = jnp.zeros_like(acc_ref)

    s = _dot_nt(h_ref[...], u_ref[...])
    act = 0.5 * s * (1.0 + lax.erf(s * (2.0 ** -0.5)))
    w = jnp.concatenate([w_ref[:, jl, :] for jl in range(_SUB)], axis=1)
    acc_ref[...] += _dot((act * w).astype(_BF), v_ref[...])

    @pl.when(j == pl.num_programs(1) - 1)
    def _():
        x2 = x_ref[...] + mod_ref[5:6, :] * acc_ref[...]
        if final:
            x2 = _rmsnorm_rows(x2, fw_ref[...])
        o_ref[...] = x2


def peer_ffn(h2, u_bf, v_bf, w, x1, mods_l, final_w, n_ctx, dec_seq, final):
    ntok, d = h2.shape
    ne = u_bf.shape[0]
    nk = w.shape[2]
    tm = 512
    te = _SUB * nk
    assert ne % te == 0 and ntok % tm == 0
    row = functools.partial(_mod_row, tm=tm, n_ctx=n_ctx, dec_seq=dec_seq)
    return pl.pallas_call(
        functools.partial(_ffn_body, final=final),
        grid=(ntok // tm, ne // te),
        in_specs=[
            pl.BlockSpec((tm, d), lambda i, j: (i, 0)),
            pl.BlockSpec((te, d), lambda i, j: (j, 0)),
            pl.BlockSpec((te, d), lambda i, j: (j, 0)),
            pl.BlockSpec((tm, _SUB, nk), lambda i, j: (i, j, 0)),
            pl.BlockSpec((tm, d), lambda i, j: (i, 0)),
            pl.BlockSpec((None, N_MOD, d), lambda i, j: (row(i), 0, 0)),
            pl.BlockSpec((1, d), lambda i, j: (0, 0)),
        ],
        out_specs=pl.BlockSpec((tm, d), lambda i, j: (i, 0)),
        out_shape=jax.ShapeDtypeStruct((ntok, d), _F32),
        scratch_shapes=[pltpu.VMEM((tm, d), _F32)],
        compiler_params=_cparams(("arbitrary", "arbitrary")),
        name="peer_ffn",
    )(h2, u_bf, v_bf, w, x1, mods_l, final_w)


def kernel(x_prompt, x_sample, state_hgrn, c, c_ctx, w_ada, b_ada, norm_w, w_in, lower_bounds, hgrn_norm_w,
           w_pool, pool_scale, w_out, w_query, sub_keys, expert_u, expert_v, final_norm_w):
    batch, seq, d = x_prompt.shape
    dec_batch, dec_seq, _ = x_sample.shape
    depth = w_ada.shape[0]
    d_a = lower_bounds.shape[2]
    heads = d_a // HEAD_DIM
    n_ctx = batch * seq
    nk = sub_keys.shape[3]
    assert dec_batch + 1 <= 8 and dec_seq % GRID_W == 0

    lb_soft = jax.nn.softmax(lower_bounds.astype(_F32), axis=1)
    lb = jnp.cumsum(lb_soft, axis=1) - lb_soft[:, :1]

    cv = jnp.zeros((8, d), _F32).at[0].set(c_ctx).at[1:1 + dec_batch].set(c)
    mods = adaln_all(cv, w_ada, b_ada)[:, :1 + dec_batch].reshape(depth, 1 + dec_batch, N_MOD, d)

    x = jnp.concatenate([x_prompt.reshape(n_ctx, d), x_sample.reshape(dec_batch * dec_seq, d)], axis=0)
    tables = _hgrn_tables(batch, seq, dec_batch, dec_seq)
    hconsts = _hgrn_constants(HGRN_CHUNK)
    pool_consts = _pool_constants(256, (seq, GRID_W))
    assert seq == 256 and (nk & (nk - 1)) == 0
    zero_state = jnp.zeros((batch, 2, heads, HEAD_DIM, HEAD_DIM), _F32)

    states = []
    for l in range(depth):
        p = in_projection(x, mods[l], norm_w[l, 0:1], w_in[l].astype(_BF), n_ctx, dec_seq)
        s0_all = jnp.concatenate([zero_state, state_hgrn[:, l].astype(_F32)], axis=0)
        o_f, o_b, s_fin = hgrn_scan(p, lb[0, l][None], lb[1, l][None], s0_all, tables, hconsts)
        states.append(s_fin[:batch])
        x1, h2 = mixer_epilogue(x, p, o_f, o_b, mods[l], hgrn_norm_w[l][None], pool_consts,
                                w_pool[l].astype(_BF), pool_scale[l][None], w_out[l].astype(_BF),
                                norm_w[l, 1:2], n_ctx, dec_seq)
        keys = sub_keys[l].reshape(-1, nk, sub_keys.shape[4]).astype(_BF)
        gate, ia, ib = peer_route(h2, w_query[l].astype(_BF), keys)
        w = peer_weights(gate, ia, ib, nk)
        x = peer_ffn(h2, expert_u[l].astype(_BF), expert_v[l].astype(_BF), w, x1, mods[l],
                     final_norm_w[None], n_ctx, dec_seq, final=(l == depth - 1))

    y_prompt = x[:n_ctx].reshape(batch, seq, d)
    y_sample = x[n_ctx:].reshape(dec_batch, dec_seq, d)
    state_new = jnp.stack(states, axis=1).astype(x_prompt.dtype)
    return (y_prompt, y_sample, state_new)
```

```python
import functools

import numpy as np
import jax
import jax.numpy as jnp
from jax import lax
from jax.experimental import pallas as pl
from jax.experimental.pallas import tpu as pltpu

GRID_W = 64
HEAD_DIM = 128
N_MOD = 6
N_A_PROJ = 5
POOL_WINDOWS = (2, 4, 8, 16)
PEER_TOPK = 16
EPS = 1e-6

SUBLANES = 8
HGRN_CHUNK = 64
HGRN_BAND = 8
HGRN_TBLOCK = 256
TM_INPROJ = 1024
TM_MIXER = 256
TM_ROUTE = 256
TM_WBUILD = 128
TM_FFN = 1024
FFN_SUB = 4
VMEM_LIMIT_BYTES = 56 * 1024 * 1024

_BF = jnp.bfloat16
_F32 = jnp.float32


def _cparams(sem):
    return pltpu.CompilerParams(dimension_semantics=sem, vmem_limit_bytes=VMEM_LIMIT_BYTES)


def _sigmoid(x):
    return 1.0 / (1.0 + jnp.exp(-x))


def _dot(a, b):
    return jnp.dot(a, b, preferred_element_type=_F32)


def _dot_nt(a, b):
    return lax.dot_general(a, b, (((1,), (1,)), ((), ())), preferred_element_type=_F32)


def _dot_tn(a, b):
    return lax.dot_general(a, b, (((0,), (0,)), ((), ())), preferred_element_type=_F32)


def _split3(x):
    x1 = x.astype(_BF)
    r = x - x1.astype(_F32)
    x2 = r.astype(_BF)
    x3 = (r - x2.astype(_F32)).astype(_BF)
    return x1, x2, x3


def _rmsnorm_rows(x, w):
    return x * lax.rsqrt(jnp.mean(x * x, axis=-1, keepdims=True) + EPS) * w


def _resident(shape, index_map):
    return pl.BlockSpec(shape, index_map, pipeline_mode=pl.Buffered(1))


def _adaln_body(cv_ref, w_ref, b_ref, o_ref):
    cv = cv_ref[...]
    s = (cv * _sigmoid(cv)).astype(_BF)
    o_ref[...] = _dot(s, w_ref[...].astype(_BF)) + b_ref[...]


def adaln_all(cv, w_ada, b_ada):
    depth, d, nm = w_ada.shape
    tn = min(nm, 1536)
    assert nm % tn == 0
    return pl.pallas_call(
        _adaln_body,
        grid=(depth, nm // tn),
        in_specs=[
            pl.BlockSpec((SUBLANES, d), lambda l, j: (0, 0)),
            pl.BlockSpec((None, d, tn), lambda l, j: (l, 0, j)),
            pl.BlockSpec((None, 1, tn), lambda l, j: (l, 0, j)),
        ],
        out_specs=pl.BlockSpec((None, SUBLANES, tn), lambda l, j: (l, 0, j)),
        out_shape=jax.ShapeDtypeStruct((depth, SUBLANES, nm), _F32),
        compiler_params=_cparams(("arbitrary", "arbitrary")),
        name="adaln",
    )(cv, w_ada, b_ada.reshape(depth, 1, nm))


def _mod_row(i, tm, n_ctx, dec_seq):
    start = i * tm
    return jnp.where(start < n_ctx, 0, 1 + (start - n_ctx) // dec_seq)


def _inproj_body(x_ref, mod_ref, nw_ref, w_ref, p_ref, h_ref):
    @pl.when(pl.program_id(1) == 0)
    def _():
        y = _rmsnorm_rows(x_ref[...], nw_ref[...])
        h_ref[...] = (y * (1.0 + mod_ref[1:2, :]) + mod_ref[0:1, :]).astype(_BF)

    p_ref[...] = _dot(h_ref[...], w_ref[...])


def in_projection(x, mods_l, nw, w_in_bf, l, n_ctx, dec_seq):
    ntok, d = x.shape
    d_in = w_in_bf.shape[2]
    tm = min(TM_INPROJ, n_ctx, dec_seq)
    tn = d_in // 4
    assert ntok % tm == 0 and n_ctx % tm == 0 and dec_seq % tm == 0 and tn % 128 == 0
    row = functools.partial(_mod_row, tm=tm, n_ctx=n_ctx, dec_seq=dec_seq)
    return pl.pallas_call(
        _inproj_body,
        grid=(ntok // tm, d_in // tn),
        in_specs=[
            pl.BlockSpec((tm, d), lambda i, j: (i, 0)),
            pl.BlockSpec((None, N_MOD, d), lambda i, j: (row(i), 0, 0)),
            pl.BlockSpec((1, d), lambda i, j: (0, 0)),
            pl.BlockSpec((None, d, tn), lambda i, j: (l, 0, j)),
        ],
        out_specs=pl.BlockSpec((tm, tn), lambda i, j: (i, j)),
        out_shape=jax.ShapeDtypeStruct((ntok, d_in), _F32),
        scratch_shapes=[pltpu.VMEM((tm, d), _BF)],
        compiler_params=_cparams(("arbitrary", "arbitrary")),
        name="inproj",
    )(x, mods_l, nw, w_in_bf)


def _hgrn_levels(c):
    r, out = HGRN_BAND, []
    while r < c:
        out.append(r)
        r *= 2
    return out


def _hgrn_constants(c):
    levels = _hgrn_levels(c)
    nl = len(levels)
    a = np.zeros((2, (1 + nl) * c, c), np.float32)
    lmask = np.zeros((2, nl, c, c), np.float32)
    rowsel = np.zeros((2, nl, c, HEAD_DIM), np.float32)
    bmask = np.zeros((2, HGRN_BAND - 1, c, HEAD_DIM), np.float32)
    for d in range(2):
        def pos(t):
            return t if d == 0 else c - 1 - t
        for t in range(c):
            pt = pos(t)
            for u in range(c):
                if pos(u) <= pt:
                    a[d, t, u] = 1.0
            for li, r in enumerate(levels):
                sb, half = pt // (2 * r), (pt // r) % 2
                ref = sb * 2 * r + r - 1
                rowsel[d, li, t, :] = float(half)
                for u in range(c):
                    pu = pos(u)
                    if half == 1 and ref < pu <= pt:
                        a[d, (1 + li) * c + t, u] = 1.0
                    if half == 0 and pt < pu <= ref:
                        a[d, (1 + li) * c + t, u] = 1.0
                    if half == 1 and pu // (2 * r) == sb and (pu // r) % 2 == 0:
                        lmask[d, li, t, u] = 1.0
            for dd in range(1, HGRN_BAND):
                if pt % HGRN_BAND >= dd:
                    bmask[d, dd - 1, t, :] = 1.0
    a3 = np.concatenate([a, a, a], axis=2)
    return (jnp.asarray(a3, _BF), jnp.asarray(lmask), jnp.asarray(rowsel), jnp.asarray(bmask))


def _hgrn_body(fwd_ref, bwd_ref, seq_ref, first_ref, last_ref,
               qf_ref, zf_ref, vf_ref, qb_ref, zb_ref, vb_ref, lbf_ref, lbb_ref, s0_ref,
               a_ref, lmask_ref, rowsel_ref, bmask_ref,
               of_ref, ob_ref, sfin_ref,
               st_ref, x_ref, k_ref):
    del fwd_ref, bwd_ref, seq_ref, last_ref
    c = HGRN_CHUNK
    tb = qf_ref.shape[0]
    nchunk = tb // c
    nl = lmask_ref.shape[1]
    scale = HEAD_DIM ** -0.5
    i = pl.program_id(1)

    @pl.when(first_ref[i] == 1)
    def _():
        st_ref[0] = s0_ref[0].T
        st_ref[1] = s0_ref[1].T

    dirs = ((qf_ref, zf_ref, vf_ref, lbf_ref, of_ref), (qb_ref, zb_ref, vb_ref, lbb_ref, ob_ref))
    for ci in range(nchunk):
        for d in range(2):
            q_ref, z_ref, v_ref, lb_ref, o_ref = dirs[d]
            r0 = (ci if d == 0 else nchunk - 1 - ci) * c
            rows = slice(r0, r0 + c)
            lb = lb_ref[...]
            f = lb + (1.0 - lb) * _sigmoid(z_ref[rows, :])
            k_ref[...] = 1.0 - f
            g1, g2, g3 = _split3(jnp.log(f))
            x_ref[...] = _dot(a_ref[d], jnp.concatenate([g1, g2, g3], axis=0))

            q = q_ref[rows, :]
            v = v_ref[rows, :]
            k = k_ref[...]
            b = x_ref[0:c, :]
            last_row = c - 1 if d == 0 else 0
            b_end = x_ref[last_row:last_row + 1, :]
            vb16 = v.astype(_BF)
            st = st_ref[d]
            o = _dot_nt((q * jnp.exp(b)).astype(_BF), st.astype(_BF))
            att = jnp.zeros((c, c), _F32)
            for li in range(nl):
                xl = x_ref[(1 + li) * c:(2 + li) * c, :]
                zl = (jnp.where(rowsel_ref[d, li] > 0.5, q, k) * jnp.exp(xl)).astype(_BF)
                att = att + jnp.where(lmask_ref[d, li] > 0.5, _dot_nt(zl, zl), 0.0)
            o = o + _dot(att.astype(_BF), vb16)
            o = o + jnp.sum(q * k, axis=-1, keepdims=True) * v
            for dd in range(1, HGRN_BAND):
                sh = dd if d == 0 else c - dd
                kd = pltpu.roll(k, sh, 0)
                vd = pltpu.roll(v, sh, 0)
                xd = b - pltpu.roll(b, sh, 0)
                w = jnp.where(bmask_ref[d, dd - 1] > 0.5, q * kd * jnp.exp(xd), 0.0)
                o = o + jnp.sum(w, axis=-1, keepdims=True) * vd
            o_ref[rows, :] = o * scale

            ke = (k * jnp.exp(b_end - b)).astype(_BF)
            st_ref[d] = st * jnp.exp(b_end) + _dot_tn(vb16, ke)

    sfin_ref[0] = st_ref[0].T
    sfin_ref[1] = st_ref[1].T


def hgrn_scan(p, lb_f, lb_b, s0_all, tables, consts):
    ntok = p.shape[0]
    n_seq, _, heads = s0_all.shape[:3]
    hd = HEAD_DIM
    tb = HGRN_TBLOCK
    n_items = tables[0].shape[0]
    a3, lmask, rowsel, bmask = consts

    def col(base, use_bwd):
        def imap(h, i, fwd, bwd, seq, first, last):
            return ((bwd if use_bwd else fwd)[i], base * heads + h)
        return pl.BlockSpec((tb, hd), imap)

    def full(arr):
        nd = arr.ndim
        return pl.BlockSpec(arr.shape, lambda h, i, *_: (0,) * nd)

    state_spec = pl.BlockSpec((None, 2, None, hd, hd),
                              lambda h, i, fwd, bwd, seq, first, last: (seq[i], 0, h, 0, 0))
    lb_spec = pl.BlockSpec((1, hd), lambda h, i, *_: (0, h))
    grid_spec = pltpu.PrefetchScalarGridSpec(
        num_scalar_prefetch=5,
        grid=(heads, n_items),
        in_specs=[col(0, False), col(1, False), col(3, False),
                  col(0, True), col(2, True), col(3, True),
                  lb_spec, lb_spec, state_spec,
                  full(a3), full(lmask), full(rowsel), full(bmask)],
        out_specs=[
            pl.BlockSpec((tb, hd), lambda h, i, fwd, bwd, seq, first, last: (fwd[i], h)),
            pl.BlockSpec((tb, hd), lambda h, i, fwd, bwd, seq, first, last: (bwd[i], h)),
            state_spec,
        ],
        scratch_shapes=[
            pltpu.VMEM((2, hd, hd), _F32),
            pltpu.VMEM((a3.shape[1], hd), _F32),
            pltpu.VMEM((HGRN_CHUNK, hd), _F32),
        ],
    )
    return pl.pallas_call(
        _hgrn_body,
        grid_spec=grid_spec,
        out_shape=[
            jax.ShapeDtypeStruct((ntok, heads * hd), _F32),
            jax.ShapeDtypeStruct((ntok, heads * hd), _F32),
            jax.ShapeDtypeStruct((n_seq, 2, heads, hd, hd), _F32),
        ],
        compiler_params=_cparams(("arbitrary", "arbitrary")),
        name="hgrn",
    )(*tables, p, p, p, p, p, p, lb_f, lb_b, s0_all, a3, lmask, rowsel, bmask)


def _hgrn_tables(batch, seq, dec_batch, dec_seq):
    tb = HGRN_TBLOCK
    assert seq % tb == 0 and dec_seq % tb == 0
    fwd, bwd, sq, first, last = [], [], [], [], []
    base = 0
    for s_idx, (nb, n) in enumerate([(seq // tb, batch), (dec_seq // tb, dec_batch)]):
        for s in range(n):
            for j in range(nb):
                fwd.append(base + j)
                bwd.append(base + nb - 1 - j)
                sq.append(s if s_idx == 0 else batch + s)
                first.append(int(j == 0))
                last.append(int(j == nb - 1))
            base += nb
    return tuple(jnp.asarray(np.array(t, np.int32)) for t in (fwd, bwd, sq, first, last))


def _pool_constants(tm, seg_lens):
    nw = len(POOL_WINDOWS)
    pw = np.zeros((len(seg_lens), nw, tm, tm), np.float32)
    cnt = np.zeros((len(seg_lens), nw, tm, 1), np.float32)
    for si, seg in enumerate(seg_lens):
        for wi, w in enumerate(POOL_WINDOWS):
            for t in range(tm):
                s0, ps = (t // seg) * seg, t % seg
                lo, hi = max(ps - w // 2, 0), min(ps + w // 2, seg)
                pw[si, wi, t, s0 + lo:s0 + hi] = 1.0
                cnt[si, wi, t, 0] = hi - lo
    pw3 = np.concatenate([pw, pw, pw], axis=3)
    return jnp.asarray(pw3, _BF), cnt


def _mix_body(x_ref, of_ref, ob_ref, g_ref, u_ref, mod_ref, gw_ref, pw_ref, cnt_ref, wpool_ref,
              pscale_ref, wout_ref, nw2_ref, x1_ref, h2_ref):
    d_a = of_ref.shape[1]
    heads = d_a // HEAD_DIM
    ngroups, gd = wpool_ref.shape[0], wpool_ref.shape[1]

    o = of_ref[...] + ob_ref[...]
    g = g_ref[...]
    gate = g * _sigmoid(g)
    pieces = []
    for h in range(heads):
        sl = slice(h * HEAD_DIM, (h + 1) * HEAD_DIM)
        pieces.append((_rmsnorm_rows(o[:, sl], gw_ref[:, sl]) * gate[:, sl]).astype(_BF))

    u = u_ref[...]
    u1, u2, u3 = _split3(u)
    us = jnp.concatenate([u1, u2, u3], axis=0)
    for gi in range(ngroups):
        sl = slice(gi * gd, (gi + 1) * gd)
        y = _dot(pw_ref[gi], us[:, sl]) / cnt_ref[gi] - u[:, sl]
        pieces.append((_dot(y.astype(_BF), wpool_ref[gi]) * pscale_ref[:, sl]).astype(_BF))

    cat = jnp.concatenate(pieces, axis=-1)
    x1 = x_ref[...] + mod_ref[2:3, :] * _dot(cat, wout_ref[...])
    x1_ref[...] = x1
    y2 = _rmsnorm_rows(x1, nw2_ref[...])
    h2_ref[...] = (y2 * (1.0 + mod_ref[4:5, :]) + mod_ref[3:4, :]).astype(_BF)


def mixer_epilogue(x, p, o_f, o_b, mods_l, gnorm_w, pool_consts, w_pool_bf, pool_scale, w_out_bf, nw2,
                   l, n_ctx, dec_seq):
    ntok, d = x.shape
    d_a = o_f.shape[1]
    d_b = p.shape[1] - N_A_PROJ * d_a
    assert d_a == d_b
    tm = TM_MIXER
    pw3, cnt = pool_consts
    cnt = jnp.broadcast_to(jnp.asarray(cnt), cnt.shape[:3] + (w_pool_bf.shape[2],))
    row = functools.partial(_mod_row, tm=tm, n_ctx=n_ctx, dec_seq=dec_seq)

    def is_sample(i):
        return jnp.where(i * tm < n_ctx, 0, 1)

    return pl.pallas_call(
        _mix_body,
        grid=(ntok // tm,),
        in_specs=[
            pl.BlockSpec((tm, d), lambda i: (i, 0)),
            pl.BlockSpec((tm, d_a), lambda i: (i, 0)),
            pl.BlockSpec((tm, d_a), lambda i: (i, 0)),
            pl.BlockSpec((tm, d_a), lambda i: (i, 4)),
            pl.BlockSpec((tm, d_b), lambda i: (i, 5)),
            pl.BlockSpec((None, N_MOD, d), lambda i: (row(i), 0, 0)),
            pl.BlockSpec((1, d_a), lambda i: (0, 0)),
            pl.BlockSpec((None,) + pw3.shape[1:], lambda i: (is_sample(i), 0, 0, 0)),
            pl.BlockSpec((None,) + cnt.shape[1:], lambda i: (is_sample(i), 0, 0, 0)),
            _resident((None,) + w_pool_bf.shape[1:], lambda i: (l, 0, 0, 0)),
            pl.BlockSpec((1, d_b), lambda i: (0, 0)),
            _resident((None,) + w_out_bf.shape[1:], lambda i: (l, 0, 0)),
            pl.BlockSpec((1, d), lambda i: (0, 0)),
        ],
        out_specs=[pl.BlockSpec((tm, d), lambda i: (i, 0)), pl.BlockSpec((tm, d), lambda i: (i, 0))],
        out_shape=[jax.ShapeDtypeStruct((ntok, d), _F32), jax.ShapeDtypeStruct((ntok, d), _BF)],
        compiler_params=_cparams(("arbitrary",)),
        name="mixer",
    )(x, o_f, o_b, p, p, mods_l, gnorm_w, pw3, cnt, w_pool_bf, pool_scale, w_out_bf, nw2)


_CAND_HALF = 8


def _cand_flat_index(topk, lanes):
    assert topk == 2 * _CAND_HALF
    flat = list(range(topk))
    for a in range(1, _CAND_HALF):
        flat += [a * topk + b for b in range(_CAND_HALF)]
    flat += [a * topk for a in range(_CAND_HALF, topk)]
    return jnp.asarray(np.broadcast_to(np.array(flat, np.int32)[:, None], (len(flat), lanes)))


def _route_body(h_ref, wq_ref, keys_ref, flat_ref, gate_ref, ia_ref, ib_ref,
                q_scr, sv_scr, si_scr, ts_scr, te_scr, g_scr, a_scr, b_scr):
    tr = h_ref.shape[0]
    nk, half = keys_ref.shape[1], keys_ref.shape[2]
    heads = keys_ref.shape[0] // 2
    topk = PEER_TOPK
    hk = _CAND_HALF
    neg = -jnp.inf

    q = _dot(h_ref[...], wq_ref[...]).astype(_BF)
    for hp in range(2 * heads):
        q_scr[hp] = q[:, hp * half:(hp + 1) * half]

    iota_k = lax.broadcasted_iota(jnp.int32, (nk, tr), 0)
    flat = flat_ref[...]
    ncand = topk * topk

    def head(h, carry):
        for p in range(2):
            s = _dot_nt(keys_ref[2 * h + p], q_scr[2 * h + p])
            for r in range(topk):
                m = jnp.max(s, axis=0, keepdims=True)
                i = jnp.min(jnp.where(s == m, iota_k, nk), axis=0, keepdims=True)
                s = jnp.where(iota_k == i, neg, s)
                sv_scr[p, r:r + 1, :] = m
                si_scr[p, r:r + 1, :] = i
        s1, s2 = sv_scr[0], sv_scr[1]
        i1, i2 = si_scr[0] * nk, si_scr[1]
        cand = [s1[0:1] + s2] + [s1[a:a + 1] + s2[0:hk] for a in range(1, hk)] + [s1[hk:] + s2[0:1]]
        cidx = [i1[0:1] + i2] + [i1[a:a + 1] + i2[0:hk] for a in range(1, hk)] + [i1[hk:] + i2[0:1]]
        cand = jnp.concatenate(cand, axis=0)
        cidx = jnp.concatenate(cidx, axis=0)
        for r in range(topk):
            m = jnp.max(cand, axis=0, keepdims=True)
            pos = jnp.min(jnp.where(cand == m, flat, ncand), axis=0, keepdims=True)
            hit = flat == pos
            te_scr[r:r + 1, :] = jnp.max(jnp.where(hit, cidx, -1), axis=0, keepdims=True)
            ts_scr[r:r + 1, :] = m
            cand = jnp.where(hit, neg, cand)
        top_s = ts_scr[...]
        e = te_scr[...]
        ex = jnp.exp(top_s - jnp.max(top_s, axis=0, keepdims=True))
        gate = ex / jnp.sum(ex, axis=0, keepdims=True)
        rows = pl.ds(pl.multiple_of(h * topk, topk), topk)
        g_scr[rows, :] = gate
        a_scr[rows, :] = lax.shift_right_logical(e, nk.bit_length() - 1).astype(_F32)
        b_scr[rows, :] = lax.bitwise_and(e, nk - 1).astype(_F32)
        return carry

    lax.fori_loop(0, heads, head, 0)
    gate_ref[...] = g_scr[...].T
    ia_ref[...] = a_scr[...].T
    ib_ref[...] = b_scr[...].T


def peer_route(h2, w_query_bf, keys_bf, l):
    ntok, d = h2.shape
    hq = w_query_bf.shape[2]
    nhp, nk, half = keys_bf.shape[1:]
    tr = TM_ROUTE
    topk = PEER_TOPK
    npair = (nhp // 2) * topk
    assert npair == nk and ntok % tr == 0
    flat = _cand_flat_index(topk, tr)
    out = jax.ShapeDtypeStruct((ntok, npair), _F32)
    spec = pl.BlockSpec((tr, npair), lambda i: (i, 0))
    return pl.pallas_call(
        _route_body,
        grid=(ntok // tr,),
        in_specs=[
            pl.BlockSpec((tr, d), lambda i: (i, 0)),
            _resident((None, d, hq), lambda i: (l, 0, 0)),
            _resident((None, nhp, nk, half), lambda i: (l, 0, 0, 0)),
            pl.BlockSpec(flat.shape, lambda i: (0, 0)),
        ],
        out_specs=[spec, spec, spec],
        out_shape=[out, out, out],
        scratch_shapes=[
            pltpu.VMEM((nhp, tr, half), _BF),
            pltpu.VMEM((2, topk, tr), _F32),
            pltpu.VMEM((2, topk, tr), jnp.int32),
            pltpu.VMEM((topk, tr), _F32),
            pltpu.VMEM((topk, tr), jnp.int32),
            pltpu.VMEM((npair, tr), _F32),
            pltpu.VMEM((npair, tr), _F32),
            pltpu.VMEM((npair, tr), _F32),
        ],
        compiler_params=_cparams(("arbitrary",)),
        name="peer_route",
    )(h2, w_query_bf, keys_bf, flat)


def _wbuild_body(g_ref, a_ref, b_ref, w_ref):
    tw, npair = g_ref.shape
    nk = w_ref.shape[1]
    iota = lax.broadcasted_iota(jnp.int32, (nk, npair), 0).astype(_F32)

    def group(gi, carry):
        t0 = pl.multiple_of(gi * SUBLANES, SUBLANES)
        g8 = g_ref[pl.ds(t0, SUBLANES), :]
        a8 = a_ref[pl.ds(t0, SUBLANES), :]
        b8 = b_ref[pl.ds(t0, SUBLANES), :]
        base = pl.multiple_of(gi * (SUBLANES * nk), SUBLANES * nk)
        for k in range(SUBLANES):
            g = g8[k:k + 1]
            g_hi = g.astype(_BF).astype(_F32)
            g_lo = g - g_hi
            hit_a = iota == a8[k:k + 1]
            hit_b = iota == b8[k:k + 1]
            at = jnp.concatenate([jnp.where(hit_a, g_hi, 0.0), jnp.where(hit_a, g_lo, 0.0)], axis=1).astype(_BF)
            bt1 = jnp.where(hit_b, 1.0, 0.0).astype(_BF)
            w_t = _dot_nt(at, jnp.concatenate([bt1, bt1], axis=1))
            w_ref[pl.ds(base + k, nk, stride=SUBLANES), :] = w_t
        return carry

    lax.fori_loop(0, tw // SUBLANES, group, 0)


def peer_weights(gate, ia, ib, nk):
    ntok, npair = gate.shape
    tw = TM_WBUILD
    spec = pl.BlockSpec((tw, npair), lambda i: (i, 0))
    w = pl.pallas_call(
        _wbuild_body,
        grid=(ntok // tw,),
        in_specs=[spec, spec, spec],
        out_specs=pl.BlockSpec((tw * nk, nk), lambda i: (i, 0)),
        out_shape=jax.ShapeDtypeStruct((ntok * nk, nk), _F32),
        compiler_params=_cparams(("arbitrary",)),
        name="peer_weights",
    )(gate, ia, ib)
    return w.reshape(ntok // SUBLANES, nk, SUBLANES, nk)


def _ffn_body(h_ref, u_ref, v_ref, w_ref, x_ref, mod_ref, fw_ref, o_ref, *, final):
    j = pl.program_id(1)
    tm = h_ref.shape[0]
    nsub, nk = w_ref.shape[1], w_ref.shape[3]

    @pl.when(j == 0)
    def _():
        o_ref[...] = jnp.zeros_like(o_ref)

    s = _dot_nt(h_ref[...], u_ref[...])
    act = 0.5 * s * (1.0 + lax.erf(s * (2.0 ** -0.5)))
    w = jnp.concatenate([w_ref[:, jl].reshape(tm, nk) for jl in range(nsub)], axis=1)
    o_ref[...] += _dot((act * w).astype(_BF), v_ref[...])

    @pl.when(j == pl.num_programs(1) - 1)
    def _():
        x2 = x_ref[...] + mod_ref[5:6, :] * o_ref[...]
        if final:
            x2 = _rmsnorm_rows(x2, fw_ref[...])
        o_ref[...] = x2


def peer_ffn(h2, u_bf, v_bf, w, x1, mods_l, final_w, l, n_ctx, dec_seq, final):
    ntok, d = h2.shape
    ne = u_bf.shape[1]
    nk = w.shape[3]
    tm = min(TM_FFN, n_ctx, dec_seq)
    te = FFN_SUB * nk
    assert ne % te == 0 and ntok % tm == 0 and n_ctx % tm == 0 and dec_seq % tm == 0
    row = functools.partial(_mod_row, tm=tm, n_ctx=n_ctx, dec_seq=dec_seq)
    once = pl.Buffered(1)
    return pl.pallas_call(
        functools.partial(_ffn_body, final=final),
        grid=(ntok // tm, ne // te),
        in_specs=[
            pl.BlockSpec((tm, d), lambda i, j: (i, 0), pipeline_mode=once),
            pl.BlockSpec((None, te, d), lambda i, j: (l, j, 0)),
            pl.BlockSpec((None, te, d), lambda i, j: (l, j, 0)),
            pl.BlockSpec((tm // SUBLANES, FFN_SUB, SUBLANES, nk), lambda i, j: (i, j, 0, 0)),
            pl.BlockSpec((tm, d), lambda i, j: (i, 0), pipeline_mode=once),
            pl.BlockSpec((None, N_MOD, d), lambda i, j: (row(i), 0, 0)),
            pl.BlockSpec((1, d), lambda i, j: (0, 0)),
        ],
        out_specs=pl.BlockSpec((tm, d), lambda i, j: (i, 0)),
        out_shape=jax.ShapeDtypeStruct((ntok, d), _F32),
        compiler_params=_cparams(("arbitrary", "arbitrary")),
        name="peer_ffn",
    )(h2, u_bf, v_bf, w, x1, mods_l, final_w)


def kernel(x_prompt, x_sample, state_hgrn, c, c_ctx, w_ada, b_ada, norm_w, w_in, lower_bounds, hgrn_norm_w,
           w_pool, pool_scale, w_out, w_query, sub_keys, expert_u, expert_v, final_norm_w):
    batch, seq, d = x_prompt.shape
    dec_batch, dec_seq, _ = x_sample.shape
    depth = w_ada.shape[0]
    d_a = lower_bounds.shape[2]
    heads = d_a // HEAD_DIM
    n_ctx = batch * seq
    nk = sub_keys.shape[3]
    assert dec_batch + 1 <= SUBLANES and dec_seq % GRID_W == 0
    assert seq == TM_MIXER and (nk & (nk - 1)) == 0

    lb_soft = jax.nn.softmax(lower_bounds.astype(_F32), axis=1)
    lb = jnp.cumsum(lb_soft, axis=1) - lb_soft[:, :1]

    cv = jnp.zeros((SUBLANES, d), _F32).at[0].set(c_ctx).at[1:1 + dec_batch].set(c)
    mods = adaln_all(cv, w_ada, b_ada)[:, :1 + dec_batch].reshape(depth, 1 + dec_batch, N_MOD, d)

    x = jnp.concatenate([x_prompt.reshape(n_ctx, d), x_sample.reshape(dec_batch * dec_seq, d)], axis=0)
    tables = _hgrn_tables(batch, seq, dec_batch, dec_seq)
    hconsts = _hgrn_constants(HGRN_CHUNK)
    pool_consts = _pool_constants(TM_MIXER, (seq, GRID_W))
    zero_state = jnp.zeros((batch, 2, heads, HEAD_DIM, HEAD_DIM), _F32)

    w_in_bf = w_in.astype(_BF)
    w_out_bf = w_out.astype(_BF)
    w_pool_bf = w_pool.astype(_BF)
    w_query_bf = w_query.astype(_BF)
    keys_bf = sub_keys.reshape(depth, -1, nk, sub_keys.shape[4]).astype(_BF)
    u_bf = expert_u.astype(_BF)
    v_bf = expert_v.astype(_BF)

    states = []
    for l in range(depth):
        p = in_projection(x, mods[l], norm_w[l, 0:1], w_in_bf, l, n_ctx, dec_seq)
        s0_all = jnp.concatenate([zero_state, state_hgrn[:, l].astype(_F32)], axis=0)
        o_f, o_b, s_fin = hgrn_scan(p, lb[0, l][None], lb[1, l][None], s0_all, tables, hconsts)
        states.append(s_fin[:batch])
        x1, h2 = mixer_epilogue(x, p, o_f, o_b, mods[l], hgrn_norm_w[l][None], pool_consts,
                                w_pool_bf, pool_scale[l][None], w_out_bf, norm_w[l, 1:2], l, n_ctx, dec_seq)
        gate, ia, ib = peer_route(h2, w_query_bf, keys_bf, l)
        w = peer_weights(gate, ia, ib, nk)
        x = peer_ffn(h2, u_bf, v_bf, w, x1, mods[l], final_norm_w[None], l, n_ctx, dec_seq,
                     final=(l == depth - 1))

    y_prompt = x[:n_ctx].reshape(batch, seq, d)
    y_sample = x[n_ctx:].reshape(dec_batch, dec_seq, d)
    state_new = jnp.stack(states, axis=1).astype(x_prompt.dtype)
    return (y_prompt, y_sample, state_new)
```

```python
import functools

import numpy as np
import jax
import jax.numpy as jnp
from jax import lax
from jax.experimental import pallas as pl
from jax.experimental.pallas import tpu as pltpu

GRID_W = 64
HEAD_DIM = 128
N_MOD = 6
N_A_PROJ = 5
POOL_WINDOWS = (2, 4, 8, 16)
PEER_TOPK = 16
EPS = 1e-6

SUBLANES = 8
HGRN_CHUNK = 64
HGRN_HEADS_PER_STEP = 4
HGRN_TBLOCK = 256
TM_INPROJ = 1024
TM_MIXER = 256
TM_ROUTE = 256
TM_WBUILD = 128
WBUILD_GROUPS_PER_ITER = 2
TM_FFN = 1024
FFN_SUB = 4
VMEM_LIMIT_BYTES = 56 * 1024 * 1024

_BF = jnp.bfloat16
_F32 = jnp.float32


def _cparams(sem):
    return pltpu.CompilerParams(dimension_semantics=sem, vmem_limit_bytes=VMEM_LIMIT_BYTES)


def _sigmoid(x):
    return 1.0 / (1.0 + jnp.exp(-x))


def _dot(a, b):
    return jnp.dot(a, b, preferred_element_type=_F32)


def _dot_nt(a, b):
    return lax.dot_general(a, b, (((1,), (1,)), ((), ())), preferred_element_type=_F32)


def _dot_tn(a, b):
    return lax.dot_general(a, b, (((0,), (0,)), ((), ())), preferred_element_type=_F32)


def _split3(x):
    x1 = x.astype(_BF)
    r = x - x1.astype(_F32)
    x2 = r.astype(_BF)
    x3 = (r - x2.astype(_F32)).astype(_BF)
    return x1, x2, x3


def _rmsnorm_rows(x, w):
    return x * lax.rsqrt(jnp.mean(x * x, axis=-1, keepdims=True) + EPS) * w


def _resident(shape, index_map):
    return pl.BlockSpec(shape, index_map, pipeline_mode=pl.Buffered(1))


def _adaln_body(cv_ref, w_ref, b_ref, o_ref):
    cv = cv_ref[...]
    s = (cv * _sigmoid(cv)).astype(_BF)
    o_ref[...] = _dot(s, w_ref[...].astype(_BF)) + b_ref[...]


def adaln_all(cv, w_ada, b_ada):
    depth, d, nm = w_ada.shape
    tn = min(nm, 1536)
    assert nm % tn == 0
    return pl.pallas_call(
        _adaln_body,
        grid=(depth, nm // tn),
        in_specs=[
            pl.BlockSpec((SUBLANES, d), lambda l, j: (0, 0)),
            pl.BlockSpec((None, d, tn), lambda l, j: (l, 0, j)),
            pl.BlockSpec((None, 1, tn), lambda l, j: (l, 0, j)),
        ],
        out_specs=pl.BlockSpec((None, SUBLANES, tn), lambda l, j: (l, 0, j)),
        out_shape=jax.ShapeDtypeStruct((depth, SUBLANES, nm), _F32),
        compiler_params=_cparams(("arbitrary", "arbitrary")),
        name="adaln",
    )(cv, w_ada, b_ada.reshape(depth, 1, nm))


def _mod_row(i, tm, n_ctx, dec_seq):
    start = i * tm
    return jnp.where(start < n_ctx, 0, 1 + (start - n_ctx) // dec_seq)


def _inproj_body(x_ref, mod_ref, nw_ref, w_ref, p_ref, h_ref):
    @pl.when(pl.program_id(1) == 0)
    def _():
        y = _rmsnorm_rows(x_ref[...], nw_ref[...])
        h_ref[...] = (y * (1.0 + mod_ref[1:2, :]) + mod_ref[0:1, :]).astype(_BF)

    p_ref[...] = _dot(h_ref[...], w_ref[...])


def in_projection(x, mods_l, nw, w_in_bf, l, n_ctx, dec_seq):
    ntok, d = x.shape
    d_in = w_in_bf.shape[2]
    tm = min(TM_INPROJ, n_ctx, dec_seq)
    tn = d_in // 4
    assert ntok % tm == 0 and n_ctx % tm == 0 and dec_seq % tm == 0 and tn % 128 == 0
    row = functools.partial(_mod_row, tm=tm, n_ctx=n_ctx, dec_seq=dec_seq)
    return pl.pallas_call(
        _inproj_body,
        grid=(ntok // tm, d_in // tn),
        in_specs=[
            pl.BlockSpec((tm, d), lambda i, j: (i, 0)),
            pl.BlockSpec((None, N_MOD, d), lambda i, j: (row(i), 0, 0)),
            pl.BlockSpec((1, d), lambda i, j: (0, 0)),
            pl.BlockSpec((None, d, tn), lambda i, j: (l, 0, j)),
        ],
        out_specs=pl.BlockSpec((tm, tn), lambda i, j: (i, j)),
        out_shape=jax.ShapeDtypeStruct((ntok, d_in), _F32),
        scratch_shapes=[pltpu.VMEM((tm, d), _BF)],
        compiler_params=_cparams(("arbitrary", "arbitrary")),
        name="inproj",
    )(x, mods_l, nw, w_in_bf)


def _hgrn_levels(c):
    r, out = 1, []
    while r < c:
        out.append(r)
        r *= 2
    return out


def _hgrn_constants(c):
    levels = _hgrn_levels(c)
    nl = len(levels)
    a = np.zeros((2, (1 + nl) * c, c), np.float32)
    lmask = np.zeros((2, nl, c, c), np.float32)
    rowsel = np.zeros((2, nl, c, HEAD_DIM), np.float32)
    for d in range(2):
        def pos(t):
            return t if d == 0 else c - 1 - t
        for t in range(c):
            pt = pos(t)
            for u in range(c):
                if pos(u) <= pt:
                    a[d, t, u] = 1.0
            for li, r in enumerate(levels):
                sb, half = pt // (2 * r), (pt // r) % 2
                ref = sb * 2 * r + r - 1
                rowsel[d, li, t, :] = float(half)
                for u in range(c):
                    pu = pos(u)
                    if half == 1 and ref < pu <= pt:
                        a[d, (1 + li) * c + t, u] = 1.0
                    if half == 0 and pt < pu <= ref:
                        a[d, (1 + li) * c + t, u] = 1.0
                    if half == 1 and pu // (2 * r) == sb and (pu // r) % 2 == 0:
                        lmask[d, li, t, u] = 1.0
    a3 = np.concatenate([a, a, a], axis=2)
    return (jnp.asarray(a3, _BF), jnp.asarray(lmask), jnp.asarray(rowsel))


def _hgrn_body(fwd_ref, bwd_ref, seq_ref, first_ref, last_ref,
               qf_ref, zf_ref, vf_ref, qb_ref, zb_ref, vb_ref, lbf_ref, lbb_ref, s0_ref,
               a_ref, lmask_ref, rowsel_ref,
               of_ref, ob_ref, sfin_ref,
               st_ref, xs_ref, ks_ref):
    del fwd_ref, bwd_ref, seq_ref, last_ref
    c = HGRN_CHUNK
    hd = HEAD_DIM
    tb = qf_ref.shape[0]
    nh = qf_ref.shape[1] // hd
    nchunk = tb // c
    nl = lmask_ref.shape[1]
    scale = hd ** -0.5
    i = pl.program_id(1)

    @pl.when(first_ref[i] == 1)
    def _():
        for d in range(2):
            for hh in range(nh):
                st_ref[d, hh] = s0_ref[d, hh].T

    dirs = ((qf_ref, zf_ref, vf_ref, lbf_ref, of_ref), (qb_ref, zb_ref, vb_ref, lbb_ref, ob_ref))
    for ci in range(nchunk):
        for d in range(2):
            q_ref, z_ref, v_ref, lb_ref, o_ref = dirs[d]
            r0 = (ci if d == 0 else nchunk - 1 - ci) * c
            rows = slice(r0, r0 + c)
            lb = lb_ref[...]
            f = lb + (1.0 - lb) * _sigmoid(z_ref[rows, :])
            x_ref = xs_ref.at[(ci % 2) * 2 + d]
            k_ref = ks_ref.at[(ci % 2) * 2 + d]
            k_ref[...] = 1.0 - f
            g1, g2, g3 = _split3(jnp.log(f))
            x_ref[...] = _dot(a_ref[d], jnp.concatenate([g1, g2, g3], axis=0))
            last_row = c - 1 if d == 0 else 0
            for hh in range(nh):
                lanes = slice(hh * hd, (hh + 1) * hd)
                q = q_ref[rows, lanes]
                v = v_ref[rows, lanes]
                k = k_ref[:, lanes]
                b = x_ref[0:c, lanes]
                b_end = x_ref[last_row:last_row + 1, lanes]
                vb16 = v.astype(_BF)
                st = st_ref[d, hh]
                o = _dot_nt((q * jnp.exp(b)).astype(_BF), st.astype(_BF))
                att = jnp.zeros((c, c), _F32)
                for li in range(nl):
                    xl = x_ref[(1 + li) * c:(2 + li) * c, lanes]
                    zl = (jnp.where(rowsel_ref[d, li] > 0.5, q, k) * jnp.exp(xl)).astype(_BF)
                    att = att + jnp.where(lmask_ref[d, li] > 0.5, _dot_nt(zl, zl), 0.0)
                o = o + _dot(att.astype(_BF), vb16)
                o = o + jnp.sum(q * k, axis=-1, keepdims=True) * v
                o_ref[rows, lanes] = o * scale
                ke = (k * jnp.exp(b_end - b)).astype(_BF)
                st_ref[d, hh] = st * jnp.exp(b_end) + _dot_tn(vb16, ke)

    for d in range(2):
        for hh in range(nh):
            sfin_ref[d, hh] = st_ref[d, hh].T


def hgrn_scan(p, lb_f, lb_b, s0_all, tables, consts):
    ntok = p.shape[0]
    n_seq, _, heads = s0_all.shape[:3]
    hd = HEAD_DIM
    nh = HGRN_HEADS_PER_STEP
    tb = HGRN_TBLOCK
    n_items = tables[0].shape[0]
    a3, lmask, rowsel = consts
    assert heads % nh == 0
    hblocks = heads // nh

    def col(base, use_bwd):
        def imap(h, i, fwd, bwd, seq, first, last):
            return ((bwd if use_bwd else fwd)[i], base * hblocks + h)
        return pl.BlockSpec((tb, nh * hd), imap)

    def full(arr):
        nd = arr.ndim
        return pl.BlockSpec(arr.shape, lambda h, i, *_: (0,) * nd)

    state_spec = pl.BlockSpec((None, 2, nh, hd, hd),
                              lambda h, i, fwd, bwd, seq, first, last: (seq[i], 0, h, 0, 0))
    lb_spec = pl.BlockSpec((1, nh * hd), lambda h, i, *_: (0, h))
    grid_spec = pltpu.PrefetchScalarGridSpec(
        num_scalar_prefetch=5,
        grid=(hblocks, n_items),
        in_specs=[col(0, False), col(1, False), col(3, False),
                  col(0, True), col(2, True), col(3, True),
                  lb_spec, lb_spec, state_spec,
                  full(a3), full(lmask), full(rowsel)],
        out_specs=[
            pl.BlockSpec((tb, nh * hd), lambda h, i, fwd, bwd, seq, first, last: (fwd[i], h)),
            pl.BlockSpec((tb, nh * hd), lambda h, i, fwd, bwd, seq, first, last: (bwd[i], h)),
            state_spec,
        ],
        scratch_shapes=[
            pltpu.VMEM((2, nh, hd, hd), _F32),
            pltpu.VMEM((4, a3.shape[1], nh * hd), _F32),
            pltpu.VMEM((4, HGRN_CHUNK, nh * hd), _F32),
        ],
    )
    return pl.pallas_call(
        _hgrn_body,
        grid_spec=grid_spec,
        out_shape=[
            jax.ShapeDtypeStruct((ntok, heads * hd), _F32),
            jax.ShapeDtypeStruct((ntok, heads * hd), _F32),
            jax.ShapeDtypeStruct((n_seq, 2, heads, hd, hd), _F32),
        ],
        compiler_params=_cparams(("arbitrary", "arbitrary")),
        name="hgrn",
    )(*tables, p, p, p, p, p, p, lb_f, lb_b, s0_all, a3, lmask, rowsel)


def _hgrn_tables(batch, seq, dec_batch, dec_seq):
    tb = HGRN_TBLOCK
    assert seq % tb == 0 and dec_seq % tb == 0
    fwd, bwd, sq, first, last = [], [], [], [], []
    base = 0
    for s_idx, (nb, n) in enumerate([(seq // tb, batch), (dec_seq // tb, dec_batch)]):
        for s in range(n):
            for j in range(nb):
                fwd.append(base + j)
                bwd.append(base + nb - 1 - j)
                sq.append(s if s_idx == 0 else batch + s)
                first.append(int(j == 0))
                last.append(int(j == nb - 1))
            base += nb
    return tuple(jnp.asarray(np.array(t, np.int32)) for t in (fwd, bwd, sq, first, last))


def _pool_constants(tm, seg_lens):
    nw = len(POOL_WINDOWS)
    pw = np.zeros((len(seg_lens), nw, tm, tm), np.float32)
    cnt = np.zeros((len(seg_lens), nw, tm, 1), np.float32)
    for si, seg in enumerate(seg_lens):
        for wi, w in enumerate(POOL_WINDOWS):
            for t in range(tm):
                s0, ps = (t // seg) * seg, t % seg
                lo, hi = max(ps - w // 2, 0), min(ps + w // 2, seg)
                pw[si, wi, t, s0 + lo:s0 + hi] = 1.0
                cnt[si, wi, t, 0] = hi - lo
    pw3 = np.concatenate([pw, pw, pw], axis=3)
    return jnp.asarray(pw3, _BF), cnt


def _mix_body(x_ref, of_ref, ob_ref, g_ref, u_ref, mod_ref, gw_ref, pw_ref, cnt_ref, wpool_ref,
              pscale_ref, wout_ref, nw2_ref, x1_ref, h2_ref):
    d_a = of_ref.shape[1]
    heads = d_a // HEAD_DIM
    ngroups, gd = wpool_ref.shape[0], wpool_ref.shape[1]

    o = of_ref[...] + ob_ref[...]
    g = g_ref[...]
    gate = g * _sigmoid(g)
    pieces = []
    for h in range(heads):
        sl = slice(h * HEAD_DIM, (h + 1) * HEAD_DIM)
        pieces.append((_rmsnorm_rows(o[:, sl], gw_ref[:, sl]) * gate[:, sl]).astype(_BF))

    u = u_ref[...]
    u1, u2, u3 = _split3(u)
    us = jnp.concatenate([u1, u2, u3], axis=0)
    for gi in range(ngroups):
        sl = slice(gi * gd, (gi + 1) * gd)
        y = _dot(pw_ref[gi], us[:, sl]) / cnt_ref[gi] - u[:, sl]
        pieces.append((_dot(y.astype(_BF), wpool_ref[gi]) * pscale_ref[:, sl]).astype(_BF))

    cat = jnp.concatenate(pieces, axis=-1)
    x1 = x_ref[...] + mod_ref[2:3, :] * _dot(cat, wout_ref[...])
    x1_ref[...] = x1
    y2 = _rmsnorm_rows(x1, nw2_ref[...])
    h2_ref[...] = (y2 * (1.0 + mod_ref[4:5, :]) + mod_ref[3:4, :]).astype(_BF)


def mixer_epilogue(x, p, o_f, o_b, mods_l, gnorm_w, pool_consts, w_pool_bf, pool_scale, w_out_bf, nw2,
                   l, n_ctx, dec_seq):
    ntok, d = x.shape
    d_a = o_f.shape[1]
    d_b = p.shape[1] - N_A_PROJ * d_a
    assert d_a == d_b
    tm = TM_MIXER
    pw3, cnt = pool_consts
    cnt = jnp.broadcast_to(jnp.asarray(cnt), cnt.shape[:3] + (w_pool_bf.shape[2],))
    row = functools.partial(_mod_row, tm=tm, n_ctx=n_ctx, dec_seq=dec_seq)

    def is_sample(i):
        return jnp.where(i * tm < n_ctx, 0, 1)

    return pl.pallas_call(
        _mix_body,
        grid=(ntok // tm,),
        in_specs=[
            pl.BlockSpec((tm, d), lambda i: (i, 0)),
            pl.BlockSpec((tm, d_a), lambda i: (i, 0)),
            pl.BlockSpec((tm, d_a), lambda i: (i, 0)),
            pl.BlockSpec((tm, d_a), lambda i: (i, 4)),
            pl.BlockSpec((tm, d_b), lambda i: (i, 5)),
            pl.BlockSpec((None, N_MOD, d), lambda i: (row(i), 0, 0)),
            pl.BlockSpec((1, d_a), lambda i: (0, 0)),
            pl.BlockSpec((None,) + pw3.shape[1:], lambda i: (is_sample(i), 0, 0, 0)),
            pl.BlockSpec((None,) + cnt.shape[1:], lambda i: (is_sample(i), 0, 0, 0)),
            _resident((None,) + w_pool_bf.shape[1:], lambda i: (l, 0, 0, 0)),
            pl.BlockSpec((1, d_b), lambda i: (0, 0)),
            _resident((None,) + w_out_bf.shape[1:], lambda i: (l, 0, 0)),
            pl.BlockSpec((1, d), lambda i: (0, 0)),
        ],
        out_specs=[pl.BlockSpec((tm, d), lambda i: (i, 0)), pl.BlockSpec((tm, d), lambda i: (i, 0))],
        out_shape=[jax.ShapeDtypeStruct((ntok, d), _F32), jax.ShapeDtypeStruct((ntok, d), _BF)],
        compiler_params=_cparams(("arbitrary",)),
        name="mixer",
    )(x, o_f, o_b, p, p, mods_l, gnorm_w, pw3, cnt, w_pool_bf, pool_scale, w_out_bf, nw2)


_CAND_HALF = 8


def _cand_flat_index(topk, lanes):
    assert topk == 2 * _CAND_HALF
    flat = list(range(topk))
    for a in range(1, _CAND_HALF):
        flat += [a * topk + b for b in range(_CAND_HALF)]
    flat += [a * topk for a in range(_CAND_HALF, topk)]
    return jnp.asarray(np.broadcast_to(np.array(flat, np.int32)[:, None], (len(flat), lanes)))


def _route_body(h_ref, wq_ref, keys_ref, flat_ref, gate_ref, ia_ref, ib_ref,
                q_scr, sv_scr, si_scr, ts_scr, te_scr, g_scr, a_scr, b_scr):
    tr = h_ref.shape[0]
    nk, half = keys_ref.shape[1], keys_ref.shape[2]
    heads = keys_ref.shape[0] // 2
    topk = PEER_TOPK
    hk = _CAND_HALF
    neg = -jnp.inf

    q = _dot(h_ref[...], wq_ref[...]).astype(_BF)
    for hp in range(2 * heads):
        q_scr[hp] = q[:, hp * half:(hp + 1) * half]

    iota_k = lax.broadcasted_iota(jnp.int32, (nk, tr), 0)
    flat = flat_ref[...]
    ncand = topk * topk

    def head(h, carry):
        for p in range(2):
            s = _dot_nt(keys_ref[2 * h + p], q_scr[2 * h + p])
            for r in range(topk):
                m = jnp.max(s, axis=0, keepdims=True)
                i = jnp.min(jnp.where(s == m, iota_k, nk), axis=0, keepdims=True)
                s = jnp.where(iota_k == i, neg, s)
                sv_scr[p, r:r + 1, :] = m
                si_scr[p, r:r + 1, :] = i
        s1, s2 = sv_scr[0], sv_scr[1]
        i1, i2 = si_scr[0] * nk, si_scr[1]
        cand = [s1[0:1] + s2] + [s1[a:a + 1] + s2[0:hk] for a in range(1, hk)] + [s1[hk:] + s2[0:1]]
        cidx = [i1[0:1] + i2] + [i1[a:a + 1] + i2[0:hk] for a in range(1, hk)] + [i1[hk:] + i2[0:1]]
        cand = jnp.concatenate(cand, axis=0)
        cidx = jnp.concatenate(cidx, axis=0)
        for r in range(topk):
            m = jnp.max(cand, axis=0, keepdims=True)
            pos = jnp.min(jnp.where(cand == m, flat, ncand), axis=0, keepdims=True)
            hit = flat == pos
            te_scr[r:r + 1, :] = jnp.max(jnp.where(hit, cidx, -1), axis=0, keepdims=True)
            ts_scr[r:r + 1, :] = m
            cand = jnp.where(hit, neg, cand)
        top_s = ts_scr[...]
        e = te_scr[...]
        ex = jnp.exp(top_s - jnp.max(top_s, axis=0, keepdims=True))
        gate = ex / jnp.sum(ex, axis=0, keepdims=True)
        rows = pl.ds(pl.multiple_of(h * topk, topk), topk)
        g_scr[rows, :] = gate
        a_scr[rows, :] = lax.shift_right_logical(e, nk.bit_length() - 1).astype(_F32)
        b_scr[rows, :] = lax.bitwise_and(e, nk - 1).astype(_F32)
        return carry

    lax.fori_loop(0, heads, head, 0)
    gate_ref[...] = g_scr[...].T
    ia_ref[...] = a_scr[...].T
    ib_ref[...] = b_scr[...].T


def peer_route(h2, w_query_bf, keys_bf, l):
    ntok, d = h2.shape
    hq = w_query_bf.shape[2]
    nhp, nk, half = keys_bf.shape[1:]
    tr = TM_ROUTE
    topk = PEER_TOPK
    npair = (nhp // 2) * topk
    assert npair == nk and ntok % tr == 0
    flat = _cand_flat_index(topk, tr)
    out = jax.ShapeDtypeStruct((ntok, npair), _F32)
    spec = pl.BlockSpec((tr, npair), lambda i: (i, 0))
    return pl.pallas_call(
        _route_body,
        grid=(ntok // tr,),
        in_specs=[
            pl.BlockSpec((tr, d), lambda i: (i, 0)),
            _resident((None, d, hq), lambda i: (l, 0, 0)),
            _resident((None, nhp, nk, half), lambda i: (l, 0, 0, 0)),
            pl.BlockSpec(flat.shape, lambda i: (0, 0)),
        ],
        out_specs=[spec, spec, spec],
        out_shape=[out, out, out],
        scratch_shapes=[
            pltpu.VMEM((nhp, tr, half), _BF),
            pltpu.VMEM((2, topk, tr), _F32),
            pltpu.VMEM((2, topk, tr), jnp.int32),
            pltpu.VMEM((topk, tr), _F32),
            pltpu.VMEM((topk, tr), jnp.int32),
            pltpu.VMEM((npair, tr), _F32),
            pltpu.VMEM((npair, tr), _F32),
            pltpu.VMEM((npair, tr), _F32),
        ],
        compiler_params=_cparams(("arbitrary",)),
        name="peer_route",
    )(h2, w_query_bf, keys_bf, flat)


def _wbuild_body(g_ref, a_ref, b_ref, w_ref):
    tw, npair = g_ref.shape
    nk = w_ref.shape[1]
    iota = lax.broadcasted_iota(jnp.int32, (nk, npair), 0).astype(_F32)

    def one_group(gi):
        t0 = pl.multiple_of(gi * SUBLANES, SUBLANES)
        g8 = g_ref[pl.ds(t0, SUBLANES), :]
        a8 = a_ref[pl.ds(t0, SUBLANES), :]
        b8 = b_ref[pl.ds(t0, SUBLANES), :]
        base = pl.multiple_of(gi * (SUBLANES * nk), SUBLANES * nk)
        for k in range(SUBLANES):
            g = g8[k:k + 1]
            g_hi = g.astype(_BF).astype(_F32)
            g_lo = g - g_hi
            hit_a = iota == a8[k:k + 1]
            hit_b = iota == b8[k:k + 1]
            at = jnp.concatenate([jnp.where(hit_a, g_hi, 0.0), jnp.where(hit_a, g_lo, 0.0)], axis=1).astype(_BF)
            bt1 = jnp.where(hit_b, 1.0, 0.0).astype(_BF)
            w_t = _dot_nt(at, jnp.concatenate([bt1, bt1], axis=1))
            w_ref[pl.ds(base + k, nk, stride=SUBLANES), :] = w_t

    def groups(it, carry):
        for u in range(WBUILD_GROUPS_PER_ITER):
            one_group(it * WBUILD_GROUPS_PER_ITER + u)
        return carry

    lax.fori_loop(0, tw // (SUBLANES * WBUILD_GROUPS_PER_ITER), groups, 0)


def peer_weights(gate, ia, ib, nk):
    ntok, npair = gate.shape
    tw = TM_WBUILD
    spec = pl.BlockSpec((tw, npair), lambda i: (i, 0))
    w = pl.pallas_call(
        _wbuild_body,
        grid=(ntok // tw,),
        in_specs=[spec, spec, spec],
        out_specs=pl.BlockSpec((tw * nk, nk), lambda i: (i, 0)),
        out_shape=jax.ShapeDtypeStruct((ntok * nk, nk), _F32),
        compiler_params=_cparams(("arbitrary",)),
        name="peer_weights",
    )(gate, ia, ib)
    return w.reshape(ntok // SUBLANES, nk, SUBLANES, nk)


def _ffn_body(h_ref, u_ref, v_ref, w_ref, x_ref, mod_ref, fw_ref, o_ref, *, final):
    j = pl.program_id(1)
    tm = h_ref.shape[0]
    nsub, nk = w_ref.shape[1], w_ref.shape[3]

    @pl.when(j == 0)
    def _():
        o_ref[...] = jnp.zeros_like(o_ref)

    s = _dot_nt(h_ref[...], u_ref[...])
    act = 0.5 * s * (1.0 + lax.erf(s * (2.0 ** -0.5)))
    w = jnp.concatenate([w_ref[:, jl].reshape(tm, nk) for jl in range(nsub)], axis=1)
    o_ref[...] += _dot((act * w).astype(_BF), v_ref[...])

    @pl.when(j == pl.num_programs(1) - 1)
    def _():
        x2 = x_ref[...] + mod_ref[5:6, :] * o_ref[...]
        if final:
            x2 = _rmsnorm_rows(x2, fw_ref[...])
        o_ref[...] = x2


def peer_ffn(h2, u_bf, v_bf, w, x1, mods_l, final_w, l, n_ctx, dec_seq, final):
    ntok, d = h2.shape
    ne = u_bf.shape[1]
    nk = w.shape[3]
    tm = min(TM_FFN, n_ctx, dec_seq)
    te = FFN_SUB * nk
    assert ne % te == 0 and ntok % tm == 0 and n_ctx % tm == 0 and dec_seq % tm == 0
    row = functools.partial(_mod_row, tm=tm, n_ctx=n_ctx, dec_seq=dec_seq)
    once = pl.Buffered(1)
    return pl.pallas_call(
        functools.partial(_ffn_body, final=final),
        grid=(ntok // tm, ne // te),
        in_specs=[
            pl.BlockSpec((tm, d), lambda i, j: (i, 0), pipeline_mode=once),
            pl.BlockSpec((None, te, d), lambda i, j: (l, j, 0)),
            pl.BlockSpec((None, te, d), lambda i, j: (l, j, 0)),
            pl.BlockSpec((tm // SUBLANES, FFN_SUB, SUBLANES, nk), lambda i, j: (i, j, 0, 0)),
            pl.BlockSpec((tm, d), lambda i, j: (i, 0), pipeline_mode=once),
            pl.BlockSpec((None, N_MOD, d), lambda i, j: (row(i), 0, 0)),
            pl.BlockSpec((1, d), lambda i, j: (0, 0)),
        ],
        out_specs=pl.BlockSpec((tm, d), lambda i, j: (i, 0)),
        out_shape=jax.ShapeDtypeStruct((ntok, d), _F32),
        compiler_params=_cparams(("arbitrary", "arbitrary")),
        name="peer_ffn",
    )(h2, u_bf, v_bf, w, x1, mods_l, final_w)


def kernel(x_prompt, x_sample, state_hgrn, c, c_ctx, w_ada, b_ada, norm_w, w_in, lower_bounds, hgrn_norm_w,
           w_pool, pool_scale, w_out, w_query, sub_keys, expert_u, expert_v, final_norm_w):
    batch, seq, d = x_prompt.shape
    dec_batch, dec_seq, _ = x_sample.shape
    depth = w_ada.shape[0]
    d_a = lower_bounds.shape[2]
    heads = d_a // HEAD_DIM
    n_ctx = batch * seq
    nk = sub_keys.shape[3]
    assert dec_batch + 1 <= SUBLANES and dec_seq % GRID_W == 0
    assert seq == TM_MIXER and (nk & (nk - 1)) == 0

    lb_soft = jax.nn.softmax(lower_bounds.astype(_F32), axis=1)
    lb = jnp.cumsum(lb_soft, axis=1) - lb_soft[:, :1]

    cv = jnp.zeros((SUBLANES, d), _F32).at[0].set(c_ctx).at[1:1 + dec_batch].set(c)
    mods = adaln_all(cv, w_ada, b_ada)[:, :1 + dec_batch].reshape(depth, 1 + dec_batch, N_MOD, d)

    x = jnp.concatenate([x_prompt.reshape(n_ctx, d), x_sample.reshape(dec_batch * dec_seq, d)], axis=0)
    tables = _hgrn_tables(batch, seq, dec_batch, dec_seq)
    hconsts = _hgrn_constants(HGRN_CHUNK)
    pool_consts = _pool_constants(TM_MIXER, (seq, GRID_W))
    zero_state = jnp.zeros((batch, 2, heads, HEAD_DIM, HEAD_DIM), _F32)

    w_in_bf = w_in.astype(_BF)
    w_out_bf = w_out.astype(_BF)
    w_pool_bf = w_pool.astype(_BF)
    w_query_bf = w_query.astype(_BF)
    keys_bf = sub_keys.reshape(depth, -1, nk, sub_keys.shape[4]).astype(_BF)
    u_bf = expert_u.astype(_BF)
    v_bf = expert_v.astype(_BF)

    states = []
    for l in range(depth):
        p = in_projection(x, mods[l], norm_w[l, 0:1], w_in_bf, l, n_ctx, dec_seq)
        s0_all = jnp.concatenate([zero_state, state_hgrn[:, l].astype(_F32)], axis=0)
        o_f, o_b, s_fin = hgrn_scan(p, lb[0, l][None], lb[1, l][None], s0_all, tables, hconsts)
        states.append(s_fin[:batch])
        x1, h2 = mixer_epilogue(x, p, o_f, o_b, mods[l], hgrn_norm_w[l][None], pool_consts,
                                w_pool_bf, pool_scale[l][None], w_out_bf, norm_w[l, 1:2], l, n_ctx, dec_seq)
        gate, ia, ib = peer_route(h2, w_query_bf, keys_bf, l)
        w = peer_weights(gate, ia, ib, nk)
        x = peer_ffn(h2, u_bf, v_bf, w, x1, mods[l], final_norm_w[None], l, n_ctx, dec_seq,
                     final=(l == depth - 1))

    y_prompt = x[:n_ctx].reshape(batch, seq, d)
    y_sample = x[n_ctx:].reshape(dec_batch, dec_seq, d)
    state_new = jnp.stack(states, axis=1).astype(x_prompt.dtype)
    return (y_prompt, y_sample, state_new)
```

```python
import functools

import numpy as np
import jax
import jax.numpy as jnp
from jax import lax
from jax.experimental import pallas as pl
from jax.experimental.pallas import tpu as pltpu

GRID_W = 64
HEAD_DIM = 128
N_MOD = 6
N_A_PROJ = 5
POOL_WINDOWS = (2, 4, 8, 16)
PEER_TOPK = 16
EPS = 1e-6

SUBLANES = 8
HGRN_CHUNK = 64
HGRN_HEADS_PER_STEP = 8
HGRN_TBLOCK = 256
TM_INPROJ = 1024
TM_MIXER = 256
TM_ROUTE = 512
TM_WBUILD = 128
WBUILD_GROUPS_PER_ITER = 4
TM_FFN = 1024
FFN_SUB = 4
VMEM_LIMIT_BYTES = 56 * 1024 * 1024

_BF = jnp.bfloat16
_F32 = jnp.float32


def _cparams(sem):
    return pltpu.CompilerParams(dimension_semantics=sem, vmem_limit_bytes=VMEM_LIMIT_BYTES)


def _sigmoid(x):
    return 1.0 / (1.0 + jnp.exp(-x))


def _dot(a, b):
    return jnp.dot(a, b, preferred_element_type=_F32)


def _dot_nt(a, b):
    return lax.dot_general(a, b, (((1,), (1,)), ((), ())), preferred_element_type=_F32)


def _dot_tn(a, b):
    return lax.dot_general(a, b, (((0,), (0,)), ((), ())), preferred_element_type=_F32)


def _split3(x):
    x1 = x.astype(_BF)
    r = x - x1.astype(_F32)
    x2 = r.astype(_BF)
    x3 = (r - x2.astype(_F32)).astype(_BF)
    return x1, x2, x3


def _rmsnorm_rows(x, w):
    return x * lax.rsqrt(jnp.mean(x * x, axis=-1, keepdims=True) + EPS) * w


def _resident(shape, index_map):
    return pl.BlockSpec(shape, index_map, pipeline_mode=pl.Buffered(1))


def _adaln_body(cv_ref, w_ref, b_ref, o_ref):
    cv = cv_ref[...]
    s = (cv * _sigmoid(cv)).astype(_BF)
    o_ref[...] = _dot(s, w_ref[...].astype(_BF)) + b_ref[...]


def adaln_all(cv, w_ada, b_ada):
    depth, d, nm = w_ada.shape
    tn = min(nm, 1536)
    assert nm % tn == 0
    return pl.pallas_call(
        _adaln_body,
        grid=(depth, nm // tn),
        in_specs=[
            pl.BlockSpec((SUBLANES, d), lambda l, j: (0, 0)),
            pl.BlockSpec((None, d, tn), lambda l, j: (l, 0, j)),
            pl.BlockSpec((None, 1, tn), lambda l, j: (l, 0, j)),
        ],
        out_specs=pl.BlockSpec((None, SUBLANES, tn), lambda l, j: (l, 0, j)),
        out_shape=jax.ShapeDtypeStruct((depth, SUBLANES, nm), _F32),
        compiler_params=_cparams(("arbitrary", "arbitrary")),
        name="adaln",
    )(cv, w_ada, b_ada.reshape(depth, 1, nm))


def _mod_row(i, tm, n_ctx, dec_seq):
    start = i * tm
    return jnp.where(start < n_ctx, 0, 1 + (start - n_ctx) // dec_seq)


def _inproj_body(x_ref, mod_ref, nw_ref, w_ref, p_ref, h_ref):
    @pl.when(pl.program_id(1) == 0)
    def _():
        y = _rmsnorm_rows(x_ref[...], nw_ref[...])
        h_ref[...] = (y * (1.0 + mod_ref[1:2, :]) + mod_ref[0:1, :]).astype(_BF)

    p_ref[...] = _dot(h_ref[...], w_ref[...])


def in_projection(x, mods_l, nw, w_in_bf, l, n_ctx, dec_seq):
    ntok, d = x.shape
    d_in = w_in_bf.shape[2]
    tm = min(TM_INPROJ, n_ctx, dec_seq)
    tn = d_in // 4
    assert ntok % tm == 0 and n_ctx % tm == 0 and dec_seq % tm == 0 and tn % 128 == 0
    row = functools.partial(_mod_row, tm=tm, n_ctx=n_ctx, dec_seq=dec_seq)
    return pl.pallas_call(
        _inproj_body,
        grid=(ntok // tm, d_in // tn),
        in_specs=[
            pl.BlockSpec((tm, d), lambda i, j: (i, 0)),
            pl.BlockSpec((None, N_MOD, d), lambda i, j: (row(i), 0, 0)),
            pl.BlockSpec((1, d), lambda i, j: (0, 0)),
            pl.BlockSpec((None, d, tn), lambda i, j: (l, 0, j)),
        ],
        out_specs=pl.BlockSpec((tm, tn), lambda i, j: (i, j)),
        out_shape=jax.ShapeDtypeStruct((ntok, d_in), _F32),
        scratch_shapes=[pltpu.VMEM((tm, d), _BF)],
        compiler_params=_cparams(("arbitrary", "arbitrary")),
        name="inproj",
    )(x, mods_l, nw, w_in_bf)


def _hgrn_levels(c):
    r, out = 1, []
    while r < c:
        out.append(r)
        r *= 2
    return out


def _hgrn_constants(c):
    levels = _hgrn_levels(c)
    nl = len(levels)
    a = np.zeros((2, (1 + nl) * c, c), np.float32)
    lmask = np.zeros((2, nl, c, c), np.float32)
    rowsel = np.zeros((2, nl, c, HEAD_DIM), np.float32)
    for d in range(2):
        def pos(t):
            return t if d == 0 else c - 1 - t
        for t in range(c):
            pt = pos(t)
            for u in range(c):
                if pos(u) <= pt:
                    a[d, t, u] = 1.0
            for li, r in enumerate(levels):
                sb, half = pt // (2 * r), (pt // r) % 2
                ref = sb * 2 * r + r - 1
                rowsel[d, li, t, :] = float(half)
                for u in range(c):
                    pu = pos(u)
                    if half == 1 and ref < pu <= pt:
                        a[d, (1 + li) * c + t, u] = 1.0
                    if half == 0 and pt < pu <= ref:
                        a[d, (1 + li) * c + t, u] = 1.0
                    if half == 1 and pu // (2 * r) == sb and (pu // r) % 2 == 0:
                        lmask[d, li, t, u] = 1.0
    a3 = np.concatenate([a, a, a], axis=2)
    return (jnp.asarray(a3, _BF), jnp.asarray(lmask), jnp.asarray(rowsel))


def _hgrn_body(fwd_ref, bwd_ref, seq_ref, first_ref, last_ref,
               qf_ref, zf_ref, vf_ref, qb_ref, zb_ref, vb_ref, lbf_ref, lbb_ref, s0_ref,
               a_ref, lmask_ref, rowsel_ref,
               of_ref, ob_ref, sfin_ref,
               st_ref, xs_ref, ks_ref):
    del fwd_ref, bwd_ref, seq_ref, last_ref
    c = HGRN_CHUNK
    hd = HEAD_DIM
    tb = qf_ref.shape[0]
    nh = qf_ref.shape[1] // hd
    nchunk = tb // c
    nl = lmask_ref.shape[1]
    scale = hd ** -0.5
    i = pl.program_id(1)

    @pl.when(first_ref[i] == 1)
    def _():
        for d in range(2):
            for hh in range(nh):
                st_ref[d, hh] = s0_ref[d, hh].T

    dirs = ((qf_ref, zf_ref, vf_ref, lbf_ref, of_ref), (qb_ref, zb_ref, vb_ref, lbb_ref, ob_ref))
    for ci in range(nchunk):
        for d in range(2):
            q_ref, z_ref, v_ref, lb_ref, o_ref = dirs[d]
            r0 = (ci if d == 0 else nchunk - 1 - ci) * c
            rows = slice(r0, r0 + c)
            lb = lb_ref[...]
            f = lb + (1.0 - lb) * _sigmoid(z_ref[rows, :])
            x_ref = xs_ref.at[(ci % 2) * 2 + d]
            k_ref = ks_ref.at[(ci % 2) * 2 + d]
            k_ref[...] = 1.0 - f
            g1, g2, g3 = _split3(jnp.log(f))
            x_ref[...] = _dot(a_ref[d], jnp.concatenate([g1, g2, g3], axis=0))
            last_row = c - 1 if d == 0 else 0
            for hh in range(nh):
                lanes = slice(hh * hd, (hh + 1) * hd)
                q = q_ref[rows, lanes]
                v = v_ref[rows, lanes]
                k = k_ref[:, lanes]
                b = x_ref[0:c, lanes]
                b_end = x_ref[last_row:last_row + 1, lanes]
                vb16 = v.astype(_BF)
                st = st_ref[d, hh]
                o = _dot_nt((q * jnp.exp(b)).astype(_BF), st.astype(_BF))
                att = jnp.zeros((c, c), _F32)
                for li in range(nl):
                    xl = x_ref[(1 + li) * c:(2 + li) * c, lanes]
                    zl = (jnp.where(rowsel_ref[d, li] > 0.5, q, k) * jnp.exp(xl)).astype(_BF)
                    att = att + jnp.where(lmask_ref[d, li] > 0.5, _dot_nt(zl, zl), 0.0)
                o = o + _dot(att.astype(_BF), vb16)
                o = o + jnp.sum(q * k, axis=-1, keepdims=True) * v
                o_ref[rows, lanes] = o * scale
                ke = (k * jnp.exp(b_end - b)).astype(_BF)
                st_ref[d, hh] = st * jnp.exp(b_end) + _dot_tn(vb16, ke)

    for d in range(2):
        for hh in range(nh):
            sfin_ref[d, hh] = st_ref[d, hh].T


def hgrn_scan(p, lb_f, lb_b, s0_all, tables, consts):
    ntok = p.shape[0]
    n_seq, _, heads = s0_all.shape[:3]
    hd = HEAD_DIM
    nh = HGRN_HEADS_PER_STEP
    tb = HGRN_TBLOCK
    n_items = tables[0].shape[0]
    a3, lmask, rowsel = consts
    assert heads % nh == 0
    hblocks = heads // nh

    def col(base, use_bwd):
        def imap(h, i, fwd, bwd, seq, first, last):
            return ((bwd if use_bwd else fwd)[i], base * hblocks + h)
        return pl.BlockSpec((tb, nh * hd), imap)

    def full(arr):
        nd = arr.ndim
        return pl.BlockSpec(arr.shape, lambda h, i, *_: (0,) * nd)

    state_spec = pl.BlockSpec((None, 2, nh, hd, hd),
                              lambda h, i, fwd, bwd, seq, first, last: (seq[i], 0, h, 0, 0))
    lb_spec = pl.BlockSpec((1, nh * hd), lambda h, i, *_: (0, h))
    grid_spec = pltpu.PrefetchScalarGridSpec(
        num_scalar_prefetch=5,
        grid=(hblocks, n_items),
        in_specs=[col(0, False), col(1, False), col(3, False),
                  col(0, True), col(2, True), col(3, True),
                  lb_spec, lb_spec, state_spec,
                  full(a3), full(lmask), full(rowsel)],
        out_specs=[
            pl.BlockSpec((tb, nh * hd), lambda h, i, fwd, bwd, seq, first, last: (fwd[i], h)),
            pl.BlockSpec((tb, nh * hd), lambda h, i, fwd, bwd, seq, first, last: (bwd[i], h)),
            state_spec,
        ],
        scratch_shapes=[
            pltpu.VMEM((2, nh, hd, hd), _F32),
            pltpu.VMEM((4, a3.shape[1], nh * hd), _F32),
            pltpu.VMEM((4, HGRN_CHUNK, nh * hd), _F32),
        ],
    )
    return pl.pallas_call(
        _hgrn_body,
        grid_spec=grid_spec,
        out_shape=[
            jax.ShapeDtypeStruct((ntok, heads * hd), _F32),
            jax.ShapeDtypeStruct((ntok, heads * hd), _F32),
            jax.ShapeDtypeStruct((n_seq, 2, heads, hd, hd), _F32),
        ],
        compiler_params=_cparams(("arbitrary", "arbitrary")),
        name="hgrn",
    )(*tables, p, p, p, p, p, p, lb_f, lb_b, s0_all, a3, lmask, rowsel)


def _hgrn_tables(batch, seq, dec_batch, dec_seq):
    tb = HGRN_TBLOCK
    assert seq % tb == 0 and dec_seq % tb == 0
    fwd, bwd, sq, first, last = [], [], [], [], []
    base = 0
    for s_idx, (nb, n) in enumerate([(seq // tb, batch), (dec_seq // tb, dec_batch)]):
        for s in range(n):
            for j in range(nb):
                fwd.append(base + j)
                bwd.append(base + nb - 1 - j)
                sq.append(s if s_idx == 0 else batch + s)
                first.append(int(j == 0))
                last.append(int(j == nb - 1))
            base += nb
    return tuple(jnp.asarray(np.array(t, np.int32)) for t in (fwd, bwd, sq, first, last))


def _pool_constants(tm, seg_lens):
    nw = len(POOL_WINDOWS)
    pw = np.zeros((len(seg_lens), nw, tm, tm), np.float32)
    cnt = np.zeros((len(seg_lens), nw, tm, 1), np.float32)
    for si, seg in enumerate(seg_lens):
        for wi, w in enumerate(POOL_WINDOWS):
            for t in range(tm):
                s0, ps = (t // seg) * seg, t % seg
                lo, hi = max(ps - w // 2, 0), min(ps + w // 2, seg)
                pw[si, wi, t, s0 + lo:s0 + hi] = 1.0
                cnt[si, wi, t, 0] = hi - lo
    pw3 = np.concatenate([pw, pw, pw], axis=3)
    return jnp.asarray(pw3, _BF), cnt


def _mix_body(x_ref, of_ref, ob_ref, g_ref, u_ref, mod_ref, gw_ref, pw_ref, cnt_ref, wpool_ref,
              pscale_ref, wout_ref, nw2_ref, x1_ref, h2_ref):
    d_a = of_ref.shape[1]
    heads = d_a // HEAD_DIM
    ngroups, gd = wpool_ref.shape[0], wpool_ref.shape[1]

    o = of_ref[...] + ob_ref[...]
    g = g_ref[...]
    gate = g * _sigmoid(g)
    pieces = []
    for h in range(heads):
        sl = slice(h * HEAD_DIM, (h + 1) * HEAD_DIM)
        pieces.append((_rmsnorm_rows(o[:, sl], gw_ref[:, sl]) * gate[:, sl]).astype(_BF))

    u = u_ref[...]
    u1, u2, u3 = _split3(u)
    us = jnp.concatenate([u1, u2, u3], axis=0)
    for gi in range(ngroups):
        sl = slice(gi * gd, (gi + 1) * gd)
        y = _dot(pw_ref[gi], us[:, sl]) / cnt_ref[gi] - u[:, sl]
        pieces.append((_dot(y.astype(_BF), wpool_ref[gi]) * pscale_ref[:, sl]).astype(_BF))

    cat = jnp.concatenate(pieces, axis=-1)
    x1 = x_ref[...] + mod_ref[2:3, :] * _dot(cat, wout_ref[...])
    x1_ref[...] = x1
    y2 = _rmsnorm_rows(x1, nw2_ref[...])
    h2_ref[...] = (y2 * (1.0 + mod_ref[4:5, :]) + mod_ref[3:4, :]).astype(_BF)


def mixer_epilogue(x, p, o_f, o_b, mods_l, gnorm_w, pool_consts, w_pool_bf, pool_scale, w_out_bf, nw2,
                   l, n_ctx, dec_seq):
    ntok, d = x.shape
    d_a = o_f.shape[1]
    d_b = p.shape[1] - N_A_PROJ * d_a
    assert d_a == d_b
    tm = TM_MIXER
    pw3, cnt = pool_consts
    cnt = jnp.broadcast_to(jnp.asarray(cnt), cnt.shape[:3] + (w_pool_bf.shape[2],))
    row = functools.partial(_mod_row, tm=tm, n_ctx=n_ctx, dec_seq=dec_seq)

    def is_sample(i):
        return jnp.where(i * tm < n_ctx, 0, 1)

    return pl.pallas_call(
        _mix_body,
        grid=(ntok // tm,),
        in_specs=[
            pl.BlockSpec((tm, d), lambda i: (i, 0)),
            pl.BlockSpec((tm, d_a), lambda i: (i, 0)),
            pl.BlockSpec((tm, d_a), lambda i: (i, 0)),
            pl.BlockSpec((tm, d_a), lambda i: (i, 4)),
            pl.BlockSpec((tm, d_b), lambda i: (i, 5)),
            pl.BlockSpec((None, N_MOD, d), lambda i: (row(i), 0, 0)),
            pl.BlockSpec((1, d_a), lambda i: (0, 0)),
            pl.BlockSpec((None,) + pw3.shape[1:], lambda i: (is_sample(i), 0, 0, 0)),
            pl.BlockSpec((None,) + cnt.shape[1:], lambda i: (is_sample(i), 0, 0, 0)),
            _resident((None,) + w_pool_bf.shape[1:], lambda i: (l, 0, 0, 0)),
            pl.BlockSpec((1, d_b), lambda i: (0, 0)),
            _resident((None,) + w_out_bf.shape[1:], lambda i: (l, 0, 0)),
            pl.BlockSpec((1, d), lambda i: (0, 0)),
        ],
        out_specs=[pl.BlockSpec((tm, d), lambda i: (i, 0)), pl.BlockSpec((tm, d), lambda i: (i, 0))],
        out_shape=[jax.ShapeDtypeStruct((ntok, d), _F32), jax.ShapeDtypeStruct((ntok, d), _BF)],
        compiler_params=_cparams(("arbitrary",)),
        name="mixer",
    )(x, o_f, o_b, p, p, mods_l, gnorm_w, pw3, cnt, w_pool_bf, pool_scale, w_out_bf, nw2)


_CAND_HALF = 8


def _cand_flat_index(topk, lanes):
    assert topk == 2 * _CAND_HALF
    flat = list(range(topk))
    for a in range(1, _CAND_HALF):
        flat += [a * topk + b for b in range(_CAND_HALF)]
    flat += [a * topk for a in range(_CAND_HALF, topk)]
    return jnp.asarray(np.broadcast_to(np.array(flat, np.int32)[:, None], (len(flat), lanes)))


def _route_body(h_ref, wq_ref, keys_ref, flat_ref, gate_ref, ia_ref, ib_ref,
                q_scr, sv_scr, si_scr, ts_scr, te_scr, g_scr, a_scr, b_scr):
    tr = h_ref.shape[0]
    nk, half = keys_ref.shape[1], keys_ref.shape[2]
    heads = keys_ref.shape[0] // 2
    topk = PEER_TOPK
    hk = _CAND_HALF
    neg = -jnp.inf

    q = _dot(h_ref[...], wq_ref[...]).astype(_BF)
    for hp in range(2 * heads):
        q_scr[hp] = q[:, hp * half:(hp + 1) * half]

    iota_k = lax.broadcasted_iota(jnp.int32, (nk, tr), 0)
    flat = flat_ref[...]
    ncand = topk * topk

    def head(h, carry):
        for p in range(2):
            s = _dot_nt(keys_ref[2 * h + p], q_scr[2 * h + p])
            for r in range(topk):
                m = jnp.max(s, axis=0, keepdims=True)
                i = jnp.min(jnp.where(s == m, iota_k, nk), axis=0, keepdims=True)
                s = jnp.where(iota_k == i, neg, s)
                sv_scr[p, r:r + 1, :] = m
                si_scr[p, r:r + 1, :] = i
        s1, s2 = sv_scr[0], sv_scr[1]
        i1, i2 = si_scr[0] * nk, si_scr[1]
        cand = [s1[0:1] + s2] + [s1[a:a + 1] + s2[0:hk] for a in range(1, hk)] + [s1[hk:] + s2[0:1]]
        cidx = [i1[0:1] + i2] + [i1[a:a + 1] + i2[0:hk] for a in range(1, hk)] + [i1[hk:] + i2[0:1]]
        cand = jnp.concatenate(cand, axis=0)
        cidx = jnp.concatenate(cidx, axis=0)
        for r in range(topk):
            m = jnp.max(cand, axis=0, keepdims=True)
            pos = jnp.min(jnp.where(cand == m, flat, ncand), axis=0, keepdims=True)
            hit = flat == pos
            te_scr[r:r + 1, :] = jnp.max(jnp.where(hit, cidx, -1), axis=0, keepdims=True)
            ts_scr[r:r + 1, :] = m
            cand = jnp.where(hit, neg, cand)
        top_s = ts_scr[...]
        e = te_scr[...]
        ex = jnp.exp(top_s - jnp.max(top_s, axis=0, keepdims=True))
        gate = ex / jnp.sum(ex, axis=0, keepdims=True)
        rows = pl.ds(pl.multiple_of(h * topk, topk), topk)
        g_scr[rows, :] = gate
        a_scr[rows, :] = lax.shift_right_logical(e, nk.bit_length() - 1).astype(_F32)
        b_scr[rows, :] = lax.bitwise_and(e, nk - 1).astype(_F32)
        return carry

    lax.fori_loop(0, heads, head, 0)
    gate_ref[...] = g_scr[...].T
    ia_ref[...] = a_scr[...].T
    ib_ref[...] = b_scr[...].T


def peer_route(h2, w_query_bf, keys_bf, l):
    ntok, d = h2.shape
    hq = w_query_bf.shape[2]
    nhp, nk, half = keys_bf.shape[1:]
    tr = TM_ROUTE
    topk = PEER_TOPK
    npair = (nhp // 2) * topk
    assert npair == nk and ntok % tr == 0
    flat = _cand_flat_index(topk, tr)
    out = jax.ShapeDtypeStruct((ntok, npair), _F32)
    spec = pl.BlockSpec((tr, npair), lambda i: (i, 0))
    return pl.pallas_call(
        _route_body,
        grid=(ntok // tr,),
        in_specs=[
            pl.BlockSpec((tr, d), lambda i: (i, 0)),
            _resident((None, d, hq), lambda i: (l, 0, 0)),
            _resident((None, nhp, nk, half), lambda i: (l, 0, 0, 0)),
            pl.BlockSpec(flat.shape, lambda i: (0, 0)),
        ],
        out_specs=[spec, spec, spec],
        out_shape=[out, out, out],
        scratch_shapes=[
            pltpu.VMEM((nhp, tr, half), _BF),
            pltpu.VMEM((2, topk, tr), _F32),
            pltpu.VMEM((2, topk, tr), jnp.int32),
            pltpu.VMEM((topk, tr), _F32),
            pltpu.VMEM((topk, tr), jnp.int32),
            pltpu.VMEM((npair, tr), _F32),
            pltpu.VMEM((npair, tr), _F32),
            pltpu.VMEM((npair, tr), _F32),
        ],
        compiler_params=_cparams(("arbitrary",)),
        name="peer_route",
    )(h2, w_query_bf, keys_bf, flat)


def _wbuild_body(g_ref, a_ref, b_ref, w_ref):
    tw, npair = g_ref.shape
    nk = w_ref.shape[1]
    iota = lax.broadcasted_iota(jnp.int32, (nk, npair), 0).astype(_F32)

    def one_group(gi):
        t0 = pl.multiple_of(gi * SUBLANES, SUBLANES)
        g8 = g_ref[pl.ds(t0, SUBLANES), :]
        a8 = a_ref[pl.ds(t0, SUBLANES), :]
        b8 = b_ref[pl.ds(t0, SUBLANES), :]
        base = pl.multiple_of(gi * (SUBLANES * nk), SUBLANES * nk)
        for k in range(SUBLANES):
            g = g8[k:k + 1]
            g_hi = g.astype(_BF).astype(_F32)
            g_lo = g - g_hi
            hit_a = iota == a8[k:k + 1]
            hit_b = iota == b8[k:k + 1]
            at = jnp.concatenate([jnp.where(hit_a, g_hi, 0.0), jnp.where(hit_a, g_lo, 0.0)], axis=1).astype(_BF)
            bt1 = jnp.where(hit_b, 1.0, 0.0).astype(_BF)
            w_t = _dot_nt(at, jnp.concatenate([bt1, bt1], axis=1))
            w_ref[pl.ds(base + k, nk, stride=SUBLANES), :] = w_t

    def groups(it, carry):
        for u in range(WBUILD_GROUPS_PER_ITER):
            one_group(it * WBUILD_GROUPS_PER_ITER + u)
        return carry

    lax.fori_loop(0, tw // (SUBLANES * WBUILD_GROUPS_PER_ITER), groups, 0)


def peer_weights(gate, ia, ib, nk):
    ntok, npair = gate.shape
    tw = TM_WBUILD
    spec = pl.BlockSpec((tw, npair), lambda i: (i, 0))
    w = pl.pallas_call(
        _wbuild_body,
        grid=(ntok // tw,),
        in_specs=[spec, spec, spec],
        out_specs=pl.BlockSpec((tw * nk, nk), lambda i: (i, 0)),
        out_shape=jax.ShapeDtypeStruct((ntok * nk, nk), _F32),
        compiler_params=_cparams(("arbitrary",)),
        name="peer_weights",
    )(gate, ia, ib)
    return w.reshape(ntok // SUBLANES, nk, SUBLANES, nk)


def _ffn_body(h_ref, u_ref, v_ref, w_ref, x_ref, mod_ref, fw_ref, o_ref, *, final):
    j = pl.program_id(1)
    tm = h_ref.shape[0]
    nsub, nk = w_ref.shape[1], w_ref.shape[3]

    @pl.when(j == 0)
    def _():
        o_ref[...] = jnp.zeros_like(o_ref)

    s = _dot_nt(h_ref[...], u_ref[...])
    act = 0.5 * s * (1.0 + lax.erf(s * (2.0 ** -0.5)))
    w = jnp.concatenate([w_ref[:, jl].reshape(tm, nk) for jl in range(nsub)], axis=1)
    o_ref[...] += _dot((act * w).astype(_BF), v_ref[...])

    @pl.when(j == pl.num_programs(1) - 1)
    def _():
        x2 = x_ref[...] + mod_ref[5:6, :] * o_ref[...]
        if final:
            x2 = _rmsnorm_rows(x2, fw_ref[...])
        o_ref[...] = x2


def peer_ffn(h2, u_bf, v_bf, w, x1, mods_l, final_w, l, n_ctx, dec_seq, final):
    ntok, d = h2.shape
    ne = u_bf.shape[1]
    nk = w.shape[3]
    tm = min(TM_FFN, n_ctx, dec_seq)
    te = FFN_SUB * nk
    assert ne % te == 0 and ntok % tm == 0 and n_ctx % tm == 0 and dec_seq % tm == 0
    row = functools.partial(_mod_row, tm=tm, n_ctx=n_ctx, dec_seq=dec_seq)
    once = pl.Buffered(1)
    return pl.pallas_call(
        functools.partial(_ffn_body, final=final),
        grid=(ntok // tm, ne // te),
        in_specs=[
            pl.BlockSpec((tm, d), lambda i, j: (i, 0), pipeline_mode=once),
            pl.BlockSpec((None, te, d), lambda i, j: (l, j, 0)),
            pl.BlockSpec((None, te, d), lambda i, j: (l, j, 0)),
            pl.BlockSpec((tm // SUBLANES, FFN_SUB, SUBLANES, nk), lambda i, j: (i, j, 0, 0)),
            pl.BlockSpec((tm, d), lambda i, j: (i, 0), pipeline_mode=once),
            pl.BlockSpec((None, N_MOD, d), lambda i, j: (row(i), 0, 0)),
            pl.BlockSpec((1, d), lambda i, j: (0, 0)),
        ],
        out_specs=pl.BlockSpec((tm, d), lambda i, j: (i, 0)),
        out_shape=jax.ShapeDtypeStruct((ntok, d), _F32),
        compiler_params=_cparams(("arbitrary", "arbitrary")),
        name="peer_ffn",
    )(h2, u_bf, v_bf, w, x1, mods_l, final_w)


def kernel(x_prompt, x_sample, state_hgrn, c, c_ctx, w_ada, b_ada, norm_w, w_in, lower_bounds, hgrn_norm_w,
           w_pool, pool_scale, w_out, w_query, sub_keys, expert_u, expert_v, final_norm_w):
    batch, seq, d = x_prompt.shape
    dec_batch, dec_seq, _ = x_sample.shape
    depth = w_ada.shape[0]
    d_a = lower_bounds.shape[2]
    heads = d_a // HEAD_DIM
    n_ctx = batch * seq
    nk = sub_keys.shape[3]
    assert dec_batch + 1 <= SUBLANES and dec_seq % GRID_W == 0
    assert seq == TM_MIXER and (nk & (nk - 1)) == 0

    lb_soft = jax.nn.softmax(lower_bounds.astype(_F32), axis=1)
    lb = jnp.cumsum(lb_soft, axis=1) - lb_soft[:, :1]

    cv = jnp.zeros((SUBLANES, d), _F32).at[0].set(c_ctx).at[1:1 + dec_batch].set(c)
    mods = adaln_all(cv, w_ada, b_ada)[:, :1 + dec_batch].reshape(depth, 1 + dec_batch, N_MOD, d)

    x = jnp.concatenate([x_prompt.reshape(n_ctx, d), x_sample.reshape(dec_batch * dec_seq, d)], axis=0)
    tables = _hgrn_tables(batch, seq, dec_batch, dec_seq)
    hconsts = _hgrn_constants(HGRN_CHUNK)
    pool_consts = _pool_constants(TM_MIXER, (seq, GRID_W))
    zero_state = jnp.zeros((batch, 2, heads, HEAD_DIM, HEAD_DIM), _F32)

    w_in_bf = w_in.astype(_BF)
    w_out_bf = w_out.astype(_BF)
    w_pool_bf = w_pool.astype(_BF)
    w_query_bf = w_query.astype(_BF)
    keys_bf = sub_keys.reshape(depth, -1, nk, sub_keys.shape[4]).astype(_BF)
    u_bf = expert_u.astype(_BF)
    v_bf = expert_v.astype(_BF)

    states = []
    for l in range(depth):
        p = in_projection(x, mods[l], norm_w[l, 0:1], w_in_bf, l, n_ctx, dec_seq)
        s0_all = jnp.concatenate([zero_state, state_hgrn[:, l].astype(_F32)], axis=0)
        o_f, o_b, s_fin = hgrn_scan(p, lb[0, l][None], lb[1, l][None], s0_all, tables, hconsts)
        states.append(s_fin[:batch])
        x1, h2 = mixer_epilogue(x, p, o_f, o_b, mods[l], hgrn_norm_w[l][None], pool_consts,
                                w_pool_bf, pool_scale[l][None], w_out_bf, norm_w[l, 1:2], l, n_ctx, dec_seq)
        gate, ia, ib = peer_route(h2, w_query_bf, keys_bf, l)
        w = peer_weights(gate, ia, ib, nk)
        x = peer_ffn(h2, u_bf, v_bf, w, x1, mods[l], final_norm_w[None], l, n_ctx, dec_seq,
                     final=(l == depth - 1))

    y_prompt = x[:n_ctx].reshape(batch, seq, d)
    y_sample = x[n_ctx:].reshape(dec_batch, dec_seq, d)
    state_new = jnp.stack(states, axis=1).astype(x_prompt.dtype)
    return (y_prompt, y_sample, state_new)
```

```python
import functools

import numpy as np
import jax
import jax.numpy as jnp
from jax import lax
from jax.experimental import pallas as pl
from jax.experimental.pallas import tpu as pltpu

GRID_W = 64
HEAD_DIM = 128
N_MOD = 6
N_A_PROJ = 5
POOL_WINDOWS = (2, 4, 8, 16)
PEER_TOPK = 16
EPS = 1e-6

SUBLANES = 8
HGRN_CHUNK = 64
HGRN_HEADS_PER_STEP = 8
HGRN_TBLOCK = 256
TM_INPROJ = 1024
TM_MIXER = 256
TM_ROUTE = 512
TM_WBUILD = 128
WBUILD_GROUPS_PER_ITER = 4
TM_FFN = 1024
FFN_SUB = 4
VMEM_LIMIT_BYTES = 56 * 1024 * 1024

_BF = jnp.bfloat16
_F32 = jnp.float32


def _cparams(sem):
    return pltpu.CompilerParams(dimension_semantics=sem, vmem_limit_bytes=VMEM_LIMIT_BYTES)


def _sigmoid(x):
    return 1.0 / (1.0 + jnp.exp(-x))


def _dot(a, b):
    return jnp.dot(a, b, preferred_element_type=_F32)


def _dot_nt(a, b):
    return lax.dot_general(a, b, (((1,), (1,)), ((), ())), preferred_element_type=_F32)


def _dot_tn(a, b):
    return lax.dot_general(a, b, (((0,), (0,)), ((), ())), preferred_element_type=_F32)


def _split3(x):
    x1 = x.astype(_BF)
    r = x - x1.astype(_F32)
    x2 = r.astype(_BF)
    x3 = (r - x2.astype(_F32)).astype(_BF)
    return x1, x2, x3


def _rmsnorm_rows(x, w):
    return x * lax.rsqrt(jnp.mean(x * x, axis=-1, keepdims=True) + EPS) * w


def _resident(shape, index_map):
    return pl.BlockSpec(shape, index_map, pipeline_mode=pl.Buffered(1))


def _adaln_body(cv_ref, w_ref, b_ref, o_ref):
    cv = cv_ref[...]
    s = (cv * _sigmoid(cv)).astype(_BF)
    o_ref[...] = _dot(s, w_ref[...].astype(_BF)) + b_ref[...]


def adaln_all(cv, w_ada, b_ada):
    depth, d, nm = w_ada.shape
    tn = min(nm, 1536)
    assert nm % tn == 0
    return pl.pallas_call(
        _adaln_body,
        grid=(depth, nm // tn),
        in_specs=[
            pl.BlockSpec((SUBLANES, d), lambda l, j: (0, 0)),
            pl.BlockSpec((None, d, tn), lambda l, j: (l, 0, j)),
            pl.BlockSpec((None, 1, tn), lambda l, j: (l, 0, j)),
        ],
        out_specs=pl.BlockSpec((None, SUBLANES, tn), lambda l, j: (l, 0, j)),
        out_shape=jax.ShapeDtypeStruct((depth, SUBLANES, nm), _F32),
        compiler_params=_cparams(("arbitrary", "arbitrary")),
        name="adaln",
    )(cv, w_ada, b_ada.reshape(depth, 1, nm))


def _mod_row(i, tm, n_ctx, dec_seq):
    start = i * tm
    return jnp.where(start < n_ctx, 0, 1 + (start - n_ctx) // dec_seq)


def _inproj_body(x_ref, mod_ref, nw_ref, w_ref, p_ref, h_ref):
    @pl.when(pl.program_id(1) == 0)
    def _():
        y = _rmsnorm_rows(x_ref[...], nw_ref[...])
        h_ref[...] = (y * (1.0 + mod_ref[1:2, :]) + mod_ref[0:1, :]).astype(_BF)

    p_ref[...] = _dot(h_ref[...], w_ref[...])


def in_projection(x, mods_l, nw, w_in_bf, l, n_ctx, dec_seq):
    ntok, d = x.shape
    d_in = w_in_bf.shape[2]
    tm = min(TM_INPROJ, n_ctx, dec_seq)
    tn = d_in // 4
    assert ntok % tm == 0 and n_ctx % tm == 0 and dec_seq % tm == 0 and tn % 128 == 0
    row = functools.partial(_mod_row, tm=tm, n_ctx=n_ctx, dec_seq=dec_seq)
    return pl.pallas_call(
        _inproj_body,
        grid=(ntok // tm, d_in // tn),
        in_specs=[
            pl.BlockSpec((tm, d), lambda i, j: (i, 0)),
            pl.BlockSpec((None, N_MOD, d), lambda i, j: (row(i), 0, 0)),
            pl.BlockSpec((1, d), lambda i, j: (0, 0)),
            pl.BlockSpec((None, d, tn), lambda i, j: (l, 0, j)),
        ],
        out_specs=pl.BlockSpec((tm, tn), lambda i, j: (i, j)),
        out_shape=jax.ShapeDtypeStruct((ntok, d_in), _F32),
        scratch_shapes=[pltpu.VMEM((tm, d), _BF)],
        compiler_params=_cparams(("arbitrary", "arbitrary")),
        name="inproj",
    )(x, mods_l, nw, w_in_bf)


def _hgrn_levels(c):
    r, out = 1, []
    while r < c:
        out.append(r)
        r *= 2
    return out


def _hgrn_constants(c):
    levels = _hgrn_levels(c)
    nl = len(levels)
    a = np.zeros((2, (1 + nl) * c, c), np.float32)
    lmask = np.zeros((2, nl, c, c), np.float32)
    rowsel = np.zeros((2, nl, c, HEAD_DIM), np.float32)
    for d in range(2):
        def pos(t):
            return t if d == 0 else c - 1 - t
        for t in range(c):
            pt = pos(t)
            for u in range(c):
                if pos(u) <= pt:
                    a[d, t, u] = 1.0
            for li, r in enumerate(levels):
                sb, half = pt // (2 * r), (pt // r) % 2
                ref = sb * 2 * r + r - 1
                rowsel[d, li, t, :] = float(half)
                for u in range(c):
                    pu = pos(u)
                    if half == 1 and ref < pu <= pt:
                        a[d, (1 + li) * c + t, u] = 1.0
                    if half == 0 and pt < pu <= ref:
                        a[d, (1 + li) * c + t, u] = 1.0
                    if half == 1 and pu // (2 * r) == sb and (pu // r) % 2 == 0:
                        lmask[d, li, t, u] = 1.0
    a3 = np.concatenate([a, a, a], axis=2)
    return (jnp.asarray(a3, _BF), jnp.asarray(lmask), jnp.asarray(rowsel))


def _hgrn_body(fwd_ref, bwd_ref, seq_ref, first_ref, last_ref,
               qf_ref, zf_ref, vf_ref, qb_ref, zb_ref, vb_ref, lbf_ref, lbb_ref, s0_ref,
               a_ref, lmask_ref, rowsel_ref,
               of_ref, ob_ref, sfin_ref,
               st_ref, xs_ref, ks_ref):
    del fwd_ref, bwd_ref, seq_ref, last_ref
    c = HGRN_CHUNK
    hd = HEAD_DIM
    tb = qf_ref.shape[0]
    nh = qf_ref.shape[1] // hd
    nchunk = tb // c
    nl = lmask_ref.shape[1]
    scale = hd ** -0.5
    i = pl.program_id(1)

    @pl.when(first_ref[i] == 1)
    def _():
        for d in range(2):
            for hh in range(nh):
                st_ref[d, hh] = s0_ref[d, hh].T

    dirs = ((qf_ref, zf_ref, vf_ref, lbf_ref, of_ref), (qb_ref, zb_ref, vb_ref, lbb_ref, ob_ref))
    for ci in range(nchunk):
        for d in range(2):
            q_ref, z_ref, v_ref, lb_ref, o_ref = dirs[d]
            r0 = (ci if d == 0 else nchunk - 1 - ci) * c
            rows = slice(r0, r0 + c)
            lb = lb_ref[...]
            f = lb + (1.0 - lb) * _sigmoid(z_ref[rows, :])
            x_ref = xs_ref.at[(ci % 2) * 2 + d]
            k_ref = ks_ref.at[(ci % 2) * 2 + d]
            k_ref[...] = 1.0 - f
            g1, g2, g3 = _split3(jnp.log(f))
            x_ref[...] = _dot(a_ref[d], jnp.concatenate([g1, g2, g3], axis=0))
            last_row = c - 1 if d == 0 else 0
            for hh in range(nh):
                lanes = slice(hh * hd, (hh + 1) * hd)
                q = q_ref[rows, lanes]
                v = v_ref[rows, lanes]
                k = k_ref[:, lanes]
                b = x_ref[0:c, lanes]
                b_end = x_ref[last_row:last_row + 1, lanes]
                vb16 = v.astype(_BF)
                st = st_ref[d, hh]
                o = _dot_nt((q * jnp.exp(b)).astype(_BF), st.astype(_BF))
                att = jnp.zeros((c, c), _F32)
                for li in range(nl):
                    xl = x_ref[(1 + li) * c:(2 + li) * c, lanes]
                    zl = (jnp.where(rowsel_ref[d, li] > 0.5, q, k) * jnp.exp(xl)).astype(_BF)
                    att = att + jnp.where(lmask_ref[d, li] > 0.5, _dot_nt(zl, zl), 0.0)
                o = o + _dot(att.astype(_BF), vb16)
                o = o + jnp.sum(q * k, axis=-1, keepdims=True) * v
                o_ref[rows, lanes] = o * scale
                ke = (k * jnp.exp(b_end - b)).astype(_BF)
                st_ref[d, hh] = st * jnp.exp(b_end) + _dot_tn(vb16, ke)

    for d in range(2):
        for hh in range(nh):
            sfin_ref[d, hh] = st_ref[d, hh].T


def hgrn_scan(p, lb_f, lb_b, s0_all, tables, consts):
    ntok = p.shape[0]
    n_seq, _, heads = s0_all.shape[:3]
    hd = HEAD_DIM
    nh = min(HGRN_HEADS_PER_STEP, heads)
    tb = HGRN_TBLOCK
    n_items = tables[0].shape[0]
    a3, lmask, rowsel = consts
    assert heads % nh == 0
    hblocks = heads // nh

    def col(base, use_bwd):
        def imap(h, i, fwd, bwd, seq, first, last):
            return ((bwd if use_bwd else fwd)[i], base * hblocks + h)
        return pl.BlockSpec((tb, nh * hd), imap)

    def full(arr):
        nd = arr.ndim
        return pl.BlockSpec(arr.shape, lambda h, i, *_: (0,) * nd)

    state_spec = pl.BlockSpec((None, 2, nh, hd, hd),
                              lambda h, i, fwd, bwd, seq, first, last: (seq[i], 0, h, 0, 0))
    lb_spec = pl.BlockSpec((1, nh * hd), lambda h, i, *_: (0, h))
    grid_spec = pltpu.PrefetchScalarGridSpec(
        num_scalar_prefetch=5,
        grid=(hblocks, n_items),
        in_specs=[col(0, False), col(1, False), col(3, False),
                  col(0, True), col(2, True), col(3, True),
                  lb_spec, lb_spec, state_spec,
                  full(a3), full(lmask), full(rowsel)],
        out_specs=[
            pl.BlockSpec((tb, nh * hd), lambda h, i, fwd, bwd, seq, first, last: (fwd[i], h)),
            pl.BlockSpec((tb, nh * hd), lambda h, i, fwd, bwd, seq, first, last: (bwd[i], h)),
            state_spec,
        ],
        scratch_shapes=[
            pltpu.VMEM((2, nh, hd, hd), _F32),
            pltpu.VMEM((4, a3.shape[1], nh * hd), _F32),
            pltpu.VMEM((4, HGRN_CHUNK, nh * hd), _F32),
        ],
    )
    return pl.pallas_call(
        _hgrn_body,
        grid_spec=grid_spec,
        out_shape=[
            jax.ShapeDtypeStruct((ntok, heads * hd), _F32),
            jax.ShapeDtypeStruct((ntok, heads * hd), _F32),
            jax.ShapeDtypeStruct((n_seq, 2, heads, hd, hd), _F32),
        ],
        compiler_params=_cparams(("arbitrary", "arbitrary")),
        name="hgrn",
    )(*tables, p, p, p, p, p, p, lb_f, lb_b, s0_all, a3, lmask, rowsel)


def _hgrn_tables(batch, seq, dec_batch, dec_seq):
    tb = HGRN_TBLOCK
    assert seq % tb == 0 and dec_seq % tb == 0
    fwd, bwd, sq, first, last = [], [], [], [], []
    base = 0
    for s_idx, (nb, n) in enumerate([(seq // tb, batch), (dec_seq // tb, dec_batch)]):
        for s in range(n):
            for j in range(nb):
                fwd.append(base + j)
                bwd.append(base + nb - 1 - j)
                sq.append(s if s_idx == 0 else batch + s)
                first.append(int(j == 0))
                last.append(int(j == nb - 1))
            base += nb
    return tuple(jnp.asarray(np.array(t, np.int32)) for t in (fwd, bwd, sq, first, last))


def _pool_constants(tm, seg_lens):
    nw = len(POOL_WINDOWS)
    pw = np.zeros((len(seg_lens), nw, tm, tm), np.float32)
    cnt = np.zeros((len(seg_lens), nw, tm, 1), np.float32)
    for si, seg in enumerate(seg_lens):
        for wi, w in enumerate(POOL_WINDOWS):
            for t in range(tm):
                s0, ps = (t // seg) * seg, t % seg
                lo, hi = max(ps - w // 2, 0), min(ps + w // 2, seg)
                pw[si, wi, t, s0 + lo:s0 + hi] = 1.0
                cnt[si, wi, t, 0] = hi - lo
    pw3 = np.concatenate([pw, pw, pw], axis=3)
    return jnp.asarray(pw3, _BF), cnt


def _mix_body(x_ref, of_ref, ob_ref, g_ref, u_ref, mod_ref, gw_ref, pw_ref, cnt_ref, wpool_ref,
              pscale_ref, wout_ref, nw2_ref, x1_ref, h2_ref):
    d_a = of_ref.shape[1]
    heads = d_a // HEAD_DIM
    ngroups, gd = wpool_ref.shape[0], wpool_ref.shape[1]

    o = of_ref[...] + ob_ref[...]
    g = g_ref[...]
    gate = g * _sigmoid(g)
    pieces = []
    for h in range(heads):
        sl = slice(h * HEAD_DIM, (h + 1) * HEAD_DIM)
        pieces.append((_rmsnorm_rows(o[:, sl], gw_ref[:, sl]) * gate[:, sl]).astype(_BF))

    u = u_ref[...]
    u1, u2, u3 = _split3(u)
    us = jnp.concatenate([u1, u2, u3], axis=0)
    for gi in range(ngroups):
        sl = slice(gi * gd, (gi + 1) * gd)
        y = _dot(pw_ref[gi], us[:, sl]) / cnt_ref[gi] - u[:, sl]
        pieces.append((_dot(y.astype(_BF), wpool_ref[gi]) * pscale_ref[:, sl]).astype(_BF))

    cat = jnp.concatenate(pieces, axis=-1)
    x1 = x_ref[...] + mod_ref[2:3, :] * _dot(cat, wout_ref[...])
    x1_ref[...] = x1
    y2 = _rmsnorm_rows(x1, nw2_ref[...])
    h2_ref[...] = (y2 * (1.0 + mod_ref[4:5, :]) + mod_ref[3:4, :]).astype(_BF)


def mixer_epilogue(x, p, o_f, o_b, mods_l, gnorm_w, pool_consts, w_pool_bf, pool_scale, w_out_bf, nw2,
                   l, n_ctx, dec_seq):
    ntok, d = x.shape
    d_a = o_f.shape[1]
    d_b = p.shape[1] - N_A_PROJ * d_a
    assert d_a == d_b
    tm = TM_MIXER
    pw3, cnt = pool_consts
    cnt = jnp.broadcast_to(jnp.asarray(cnt), cnt.shape[:3] + (w_pool_bf.shape[2],))
    row = functools.partial(_mod_row, tm=tm, n_ctx=n_ctx, dec_seq=dec_seq)

    def is_sample(i):
        return jnp.where(i * tm < n_ctx, 0, 1)

    return pl.pallas_call(
        _mix_body,
        grid=(ntok // tm,),
        in_specs=[
            pl.BlockSpec((tm, d), lambda i: (i, 0)),
            pl.BlockSpec((tm, d_a), lambda i: (i, 0)),
            pl.BlockSpec((tm, d_a), lambda i: (i, 0)),
            pl.BlockSpec((tm, d_a), lambda i: (i, 4)),
            pl.BlockSpec((tm, d_b), lambda i: (i, 5)),
            pl.BlockSpec((None, N_MOD, d), lambda i: (row(i), 0, 0)),
            pl.BlockSpec((1, d_a), lambda i: (0, 0)),
            pl.BlockSpec((None,) + pw3.shape[1:], lambda i: (is_sample(i), 0, 0, 0)),
            pl.BlockSpec((None,) + cnt.shape[1:], lambda i: (is_sample(i), 0, 0, 0)),
            _resident((None,) + w_pool_bf.shape[1:], lambda i: (l, 0, 0, 0)),
            pl.BlockSpec((1, d_b), lambda i: (0, 0)),
            _resident((None,) + w_out_bf.shape[1:], lambda i: (l, 0, 0)),
            pl.BlockSpec((1, d), lambda i: (0, 0)),
        ],
        out_specs=[pl.BlockSpec((tm, d), lambda i: (i, 0)), pl.BlockSpec((tm, d), lambda i: (i, 0))],
        out_shape=[jax.ShapeDtypeStruct((ntok, d), _F32), jax.ShapeDtypeStruct((ntok, d), _BF)],
        compiler_params=_cparams(("arbitrary",)),
        name="mixer",
    )(x, o_f, o_b, p, p, mods_l, gnorm_w, pw3, cnt, w_pool_bf, pool_scale, w_out_bf, nw2)


_CAND_HALF = 8


def _cand_flat_index(topk, lanes):
    assert topk == 2 * _CAND_HALF
    flat = list(range(topk))
    for a in range(1, _CAND_HALF):
        flat += [a * topk + b for b in range(_CAND_HALF)]
    flat += [a * topk for a in range(_CAND_HALF, topk)]
    return jnp.asarray(np.broadcast_to(np.array(flat, np.float32)[:, None], (len(flat), lanes)))


def _route_body(h_ref, wq_ref, keys_ref, flat_ref, gate_ref, ia_ref, ib_ref,
                q_scr, sv_scr, si_scr, ts_scr, te_scr, g_scr, a_scr, b_scr):
    tr = h_ref.shape[0]
    nk, half = keys_ref.shape[1], keys_ref.shape[2]
    heads = keys_ref.shape[0] // 2
    topk = PEER_TOPK
    hk = _CAND_HALF
    neg = -jnp.inf

    q = _dot(h_ref[...], wq_ref[...]).astype(_BF)
    for hp in range(2 * heads):
        q_scr[hp] = q[:, hp * half:(hp + 1) * half]

    iota_k = lax.broadcasted_iota(jnp.int32, (nk, tr), 0).astype(_F32)
    flat = flat_ref[...]
    ncand = float(topk * topk)

    def head(h, carry):
        for p in range(2):
            s = _dot_nt(keys_ref[2 * h + p], q_scr[2 * h + p])
            for r in range(topk):
                m = jnp.max(s, axis=0, keepdims=True)
                i = jnp.min(jnp.where(s == m, iota_k, float(nk)), axis=0, keepdims=True)
                s = jnp.where(iota_k == i, neg, s)
                sv_scr[p, r:r + 1, :] = m
                si_scr[p, r:r + 1, :] = i
        s1, s2 = sv_scr[0], sv_scr[1]
        i1, i2 = si_scr[0] * float(nk), si_scr[1]
        cand = [s1[0:1] + s2] + [s1[a:a + 1] + s2[0:hk] for a in range(1, hk)] + [s1[hk:] + s2[0:1]]
        cidx = [i1[0:1] + i2] + [i1[a:a + 1] + i2[0:hk] for a in range(1, hk)] + [i1[hk:] + i2[0:1]]
        cand = jnp.concatenate(cand, axis=0)
        ckey = flat * float(nk * nk) + jnp.concatenate(cidx, axis=0)
        for r in range(topk):
            m = jnp.max(cand, axis=0, keepdims=True)
            kmin = jnp.min(jnp.where(cand == m, ckey, ncand * nk * nk), axis=0, keepdims=True)
            te_scr[r:r + 1, :] = kmin
            ts_scr[r:r + 1, :] = m
            cand = jnp.where(ckey == kmin, neg, cand)
        top_s = ts_scr[...]
        e = te_scr[...]
        e = e - jnp.floor(e * (1.0 / (nk * nk))) * float(nk * nk)
        ex = jnp.exp(top_s - jnp.max(top_s, axis=0, keepdims=True))
        gate = ex / jnp.sum(ex, axis=0, keepdims=True)
        rows = pl.ds(pl.multiple_of(h * topk, topk), topk)
        g_scr[rows, :] = gate
        first_key = jnp.floor(e * (1.0 / nk))
        a_scr[rows, :] = first_key
        b_scr[rows, :] = e - first_key * float(nk)
        return carry

    lax.fori_loop(0, heads, head, 0)
    gate_ref[...] = g_scr[...].T
    ia_ref[...] = a_scr[...].T
    ib_ref[...] = b_scr[...].T


def peer_route(h2, w_query_bf, keys_bf, l):
    ntok, d = h2.shape
    hq = w_query_bf.shape[2]
    nhp, nk, half = keys_bf.shape[1:]
    tr = TM_ROUTE
    topk = PEER_TOPK
    npair = (nhp // 2) * topk
    assert npair == nk and ntok % tr == 0
    flat = _cand_flat_index(topk, tr)
    out = jax.ShapeDtypeStruct((ntok, npair), _F32)
    spec = pl.BlockSpec((tr, npair), lambda i: (i, 0))
    return pl.pallas_call(
        _route_body,
        grid=(ntok // tr,),
        in_specs=[
            pl.BlockSpec((tr, d), lambda i: (i, 0)),
            _resident((None, d, hq), lambda i: (l, 0, 0)),
            _resident((None, nhp, nk, half), lambda i: (l, 0, 0, 0)),
            pl.BlockSpec(flat.shape, lambda i: (0, 0)),
        ],
        out_specs=[spec, spec, spec],
        out_shape=[out, out, out],
        scratch_shapes=[
            pltpu.VMEM((nhp, tr, half), _BF),
            pltpu.VMEM((2, topk, tr), _F32),
            pltpu.VMEM((2, topk, tr), _F32),
            pltpu.VMEM((topk, tr), _F32),
            pltpu.VMEM((topk, tr), _F32),
            pltpu.VMEM((npair, tr), _F32),
            pltpu.VMEM((npair, tr), _F32),
            pltpu.VMEM((npair, tr), _F32),
        ],
        compiler_params=_cparams(("arbitrary",)),
        name="peer_route",
    )(h2, w_query_bf, keys_bf, flat)


def _wbuild_body(g_ref, a_ref, b_ref, w_ref):
    tw, npair = g_ref.shape
    nk = w_ref.shape[1]
    iota = lax.broadcasted_iota(jnp.int32, (nk, npair), 0).astype(_F32)

    def one_group(gi):
        t0 = pl.multiple_of(gi * SUBLANES, SUBLANES)
        g8 = g_ref[pl.ds(t0, SUBLANES), :]
        a8 = a_ref[pl.ds(t0, SUBLANES), :]
        b8 = b_ref[pl.ds(t0, SUBLANES), :]
        base = pl.multiple_of(gi * (SUBLANES * nk), SUBLANES * nk)
        for k in range(SUBLANES):
            g = g8[k:k + 1]
            g_hi = g.astype(_BF).astype(_F32)
            g_lo = g - g_hi
            hit_a = iota == a8[k:k + 1]
            hit_b = iota == b8[k:k + 1]
            at = jnp.concatenate([jnp.where(hit_a, g_hi, 0.0), jnp.where(hit_a, g_lo, 0.0)], axis=1).astype(_BF)
            bt1 = jnp.where(hit_b, 1.0, 0.0).astype(_BF)
            w_t = _dot_nt(at, jnp.concatenate([bt1, bt1], axis=1))
            w_ref[pl.ds(base + k, nk, stride=SUBLANES), :] = w_t

    def groups(it, carry):
        for u in range(WBUILD_GROUPS_PER_ITER):
            one_group(it * WBUILD_GROUPS_PER_ITER + u)
        return carry

    lax.fori_loop(0, tw // (SUBLANES * WBUILD_GROUPS_PER_ITER), groups, 0)


def peer_weights(gate, ia, ib, nk):
    ntok, npair = gate.shape
    tw = TM_WBUILD
    spec = pl.BlockSpec((tw, npair), lambda i: (i, 0))
    w = pl.pallas_call(
        _wbuild_body,
        grid=(ntok // tw,),
        in_specs=[spec, spec, spec],
        out_specs=pl.BlockSpec((tw * nk, nk), lambda i: (i, 0)),
        out_shape=jax.ShapeDtypeStruct((ntok * nk, nk), _F32),
        compiler_params=_cparams(("arbitrary",)),
        name="peer_weights",
    )(gate, ia, ib)
    return w.reshape(ntok // SUBLANES, nk, SUBLANES, nk)


def _ffn_body(h_ref, u_ref, v_ref, w_ref, x_ref, mod_ref, fw_ref, o_ref, *, final):
    j = pl.program_id(1)
    tm = h_ref.shape[0]
    nsub, nk = w_ref.shape[1], w_ref.shape[3]

    @pl.when(j == 0)
    def _():
        o_ref[...] = jnp.zeros_like(o_ref)

    s = _dot_nt(h_ref[...], u_ref[...].astype(_BF))
    act = 0.5 * s * (1.0 + lax.erf(s * (2.0 ** -0.5)))
    w = jnp.concatenate([w_ref[:, jl].reshape(tm, nk) for jl in range(nsub)], axis=1)
    o_ref[...] += _dot((act * w).astype(_BF), v_ref[...].astype(_BF))

    @pl.when(j == pl.num_programs(1) - 1)
    def _():
        x2 = x_ref[...] + mod_ref[5:6, :] * o_ref[...]
        if final:
            x2 = _rmsnorm_rows(x2, fw_ref[...])
        o_ref[...] = x2


def peer_ffn(h2, u_bf, v_bf, w, x1, mods_l, final_w, l, n_ctx, dec_seq, final):
    ntok, d = h2.shape
    ne = u_bf.shape[1]
    nk = w.shape[3]
    tm = min(TM_FFN, n_ctx, dec_seq)
    te = FFN_SUB * nk
    assert ne % te == 0 and ntok % tm == 0 and n_ctx % tm == 0 and dec_seq % tm == 0
    row = functools.partial(_mod_row, tm=tm, n_ctx=n_ctx, dec_seq=dec_seq)
    once = pl.Buffered(1)
    return pl.pallas_call(
        functools.partial(_ffn_body, final=final),
        grid=(ntok // tm, ne // te),
        in_specs=[
            pl.BlockSpec((tm, d), lambda i, j: (i, 0), pipeline_mode=once),
            pl.BlockSpec((None, te, d), lambda i, j: (l, j, 0)),
            pl.BlockSpec((None, te, d), lambda i, j: (l, j, 0)),
            pl.BlockSpec((tm // SUBLANES, FFN_SUB, SUBLANES, nk), lambda i, j: (i, j, 0, 0)),
            pl.BlockSpec((tm, d), lambda i, j: (i, 0), pipeline_mode=once),
            pl.BlockSpec((None, N_MOD, d), lambda i, j: (row(i), 0, 0)),
            pl.BlockSpec((1, d), lambda i, j: (0, 0)),
        ],
        out_specs=pl.BlockSpec((tm, d), lambda i, j: (i, 0), pipeline_mode=once),
        out_shape=jax.ShapeDtypeStruct((ntok, d), _F32),
        compiler_params=_cparams(("arbitrary", "arbitrary")),
        name="peer_ffn",
    )(h2, u_bf, v_bf, w, x1, mods_l, final_w)


def kernel(x_prompt, x_sample, state_hgrn, c, c_ctx, w_ada, b_ada, norm_w, w_in, lower_bounds, hgrn_norm_w,
           w_pool, pool_scale, w_out, w_query, sub_keys, expert_u, expert_v, final_norm_w):
    batch, seq, d = x_prompt.shape
    dec_batch, dec_seq, _ = x_sample.shape
    depth = w_ada.shape[0]
    d_a = lower_bounds.shape[2]
    heads = d_a // HEAD_DIM
    n_ctx = batch * seq
    nk = sub_keys.shape[3]
    assert dec_batch + 1 <= SUBLANES and dec_seq % GRID_W == 0
    assert seq == TM_MIXER and (nk & (nk - 1)) == 0

    lb_soft = jax.nn.softmax(lower_bounds.astype(_F32), axis=1)
    lb = jnp.cumsum(lb_soft, axis=1) - lb_soft[:, :1]

    cv = jnp.zeros((SUBLANES, d), _F32).at[0].set(c_ctx).at[1:1 + dec_batch].set(c)
    mods = adaln_all(cv, w_ada, b_ada)[:, :1 + dec_batch].reshape(depth, 1 + dec_batch, N_MOD, d)

    x = jnp.concatenate([x_prompt.reshape(n_ctx, d), x_sample.reshape(dec_batch * dec_seq, d)], axis=0)
    tables = _hgrn_tables(batch, seq, dec_batch, dec_seq)
    hconsts = _hgrn_constants(HGRN_CHUNK)
    pool_consts = _pool_constants(TM_MIXER, (seq, GRID_W))
    zero_state = jnp.zeros((batch, 2, heads, HEAD_DIM, HEAD_DIM), _F32)

    w_in_bf = w_in.astype(_BF)
    w_out_bf = w_out.astype(_BF)
    w_pool_bf = w_pool.astype(_BF)
    w_query_bf = w_query.astype(_BF)
    keys_bf = sub_keys.reshape(depth, -1, nk, sub_keys.shape[4]).astype(_BF)
    u_bf, v_bf = expert_u, expert_v

    states = []
    for l in range(depth):
        p = in_projection(x, mods[l], norm_w[l, 0:1], w_in_bf, l, n_ctx, dec_seq)
        s0_all = jnp.concatenate([zero_state, state_hgrn[:, l].astype(_F32)], axis=0)
        o_f, o_b, s_fin = hgrn_scan(p, lb[0, l][None], lb[1, l][None], s0_all, tables, hconsts)
        states.append(s_fin[:batch])
        x1, h2 = mixer_epilogue(x, p, o_f, o_b, mods[l], hgrn_norm_w[l][None], pool_consts,
                                w_pool_bf, pool_scale[l][None], w_out_bf, norm_w[l, 1:2], l, n_ctx, dec_seq)
        gate, ia, ib = peer_route(h2, w_query_bf, keys_bf, l)
        w = peer_weights(gate, ia, ib, nk)
        x = peer_ffn(h2, u_bf, v_bf, w, x1, mods[l], final_norm_w[None], l, n_ctx, dec_seq,
                     final=(l == depth - 1))

    y_prompt = x[:n_ctx].reshape(batch, seq, d)
    y_sample = x[n_ctx:].reshape(dec_batch, dec_seq, d)
    state_new = jnp.stack(states, axis=1).astype(x_prompt.dtype)
    return (y_prompt, y_sample, state_new)
```

```python
import functools

import numpy as np
import jax
import jax.numpy as jnp
from jax import lax
from jax.experimental import pallas as pl
from jax.experimental.pallas import tpu as pltpu

GRID_W = 64
HEAD_DIM = 128
N_MOD = 6
N_A_PROJ = 5
POOL_WINDOWS = (2, 4, 8, 16)
PEER_TOPK = 16
EPS = 1e-6
LOG2_E = 1.4426950408889634

SUBLANES = 8
HGRN_CHUNK = 64
HGRN_HEADS_PER_STEP = 8
HGRN_TBLOCK = 256
TM_INPROJ = 1024
INPROJ_COL_TILES = 8
TM_MIXER = 256
TM_ROUTE = 1024
TM_WBUILD = 128
WBUILD_GROUPS_PER_ITER = 4
TM_FFN = 1024
FFN_SUB = 4
VMEM_LIMIT_BYTES = 56 * 1024 * 1024

_BF = jnp.bfloat16
_F32 = jnp.float32


def _cparams(sem):
    return pltpu.CompilerParams(dimension_semantics=sem, vmem_limit_bytes=VMEM_LIMIT_BYTES)


def _sigmoid(x):
    return 1.0 / (1.0 + jnp.exp(-x))


def _dot(a, b):
    return jnp.dot(a, b, preferred_element_type=_F32)


def _dot_nt(a, b):
    return lax.dot_general(a, b, (((1,), (1,)), ((), ())), preferred_element_type=_F32)


def _dot_tn(a, b):
    return lax.dot_general(a, b, (((0,), (0,)), ((), ())), preferred_element_type=_F32)


def _split3(x):
    x1 = x.astype(_BF)
    r = x - x1.astype(_F32)
    x2 = r.astype(_BF)
    x3 = (r - x2.astype(_F32)).astype(_BF)
    return x1, x2, x3


def _rmsnorm_rows(x, w):
    return x * lax.rsqrt(jnp.mean(x * x, axis=-1, keepdims=True) + EPS) * w


def _resident(shape, index_map):
    return pl.BlockSpec(shape, index_map, pipeline_mode=pl.Buffered(1))


def _adaln_body(cv_ref, w_ref, b_ref, o_ref):
    cv = cv_ref[...]
    s = (cv * _sigmoid(cv)).astype(_BF)
    o_ref[...] = _dot(s, w_ref[...].astype(_BF)) + b_ref[...]


def adaln_all(cv, w_ada, b_ada):
    depth, d, nm = w_ada.shape
    tn = min(nm, 1536)
    assert nm % tn == 0
    return pl.pallas_call(
        _adaln_body,
        grid=(depth, nm // tn),
        in_specs=[
            pl.BlockSpec((SUBLANES, d), lambda l, j: (0, 0)),
            pl.BlockSpec((None, d, tn), lambda l, j: (l, 0, j)),
            pl.BlockSpec((None, 1, tn), lambda l, j: (l, 0, j)),
        ],
        out_specs=pl.BlockSpec((None, SUBLANES, tn), lambda l, j: (l, 0, j)),
        out_shape=jax.ShapeDtypeStruct((depth, SUBLANES, nm), _F32),
        compiler_params=_cparams(("arbitrary", "arbitrary")),
        name="adaln",
    )(cv, w_ada, b_ada.reshape(depth, 1, nm))


def _mod_row(i, tm, n_ctx, dec_seq):
    start = i * tm
    return jnp.where(start < n_ctx, 0, 1 + (start - n_ctx) // dec_seq)


def _inproj_body(x_ref, mod_ref, nw_ref, w_ref, p_ref, h_ref):
    @pl.when(pl.program_id(1) == 0)
    def _():
        y = _rmsnorm_rows(x_ref[...], nw_ref[...])
        h_ref[...] = (y * (1.0 + mod_ref[1:2, :]) + mod_ref[0:1, :]).astype(_BF)

    p_ref[...] = _dot(h_ref[...], w_ref[...].astype(_BF))


def in_projection(x, mods_l, nw, w_in_bf, l, n_ctx, dec_seq):
    ntok, d = x.shape
    d_in = w_in_bf.shape[2]
    tm = min(TM_INPROJ, n_ctx, dec_seq)
    tn = d_in // INPROJ_COL_TILES
    assert ntok % tm == 0 and n_ctx % tm == 0 and dec_seq % tm == 0 and tn % 128 == 0
    row = functools.partial(_mod_row, tm=tm, n_ctx=n_ctx, dec_seq=dec_seq)
    return pl.pallas_call(
        _inproj_body,
        grid=(ntok // tm, d_in // tn),
        in_specs=[
            pl.BlockSpec((tm, d), lambda i, j: (i, 0)),
            pl.BlockSpec((None, N_MOD, d), lambda i, j: (row(i), 0, 0)),
            pl.BlockSpec((1, d), lambda i, j: (0, 0)),
            pl.BlockSpec((None, d, tn), lambda i, j: (l, 0, j)),
        ],
        out_specs=pl.BlockSpec((tm, tn), lambda i, j: (i, j)),
        out_shape=jax.ShapeDtypeStruct((ntok, d_in), _F32),
        scratch_shapes=[pltpu.VMEM((tm, d), _BF)],
        compiler_params=_cparams(("arbitrary", "arbitrary")),
        name="inproj",
    )(x, mods_l, nw, w_in_bf)


def _hgrn_levels(c):
    r, out = 1, []
    while r < c:
        out.append(r)
        r *= 2
    return out


def _hgrn_constants(c):
    levels = _hgrn_levels(c)
    nl = len(levels)
    a = np.zeros((2, (1 + nl) * c, c), np.float32)
    lmask = np.zeros((2, nl, c, c), np.float32)
    rowsel = np.zeros((2, nl, c, HEAD_DIM), np.float32)
    for d in range(2):
        def pos(t):
            return t if d == 0 else c - 1 - t
        for t in range(c):
            pt = pos(t)
            for u in range(c):
                if pos(u) <= pt:
                    a[d, t, u] = 1.0
            for li, r in enumerate(levels):
                sb, half = pt // (2 * r), (pt // r) % 2
                ref = sb * 2 * r + r - 1
                rowsel[d, li, t, :] = float(half)
                for u in range(c):
                    pu = pos(u)
                    if half == 1 and ref < pu <= pt:
                        a[d, (1 + li) * c + t, u] = 1.0
                    if half == 0 and pt < pu <= ref:
                        a[d, (1 + li) * c + t, u] = 1.0
                    if half == 1 and pu // (2 * r) == sb and (pu // r) % 2 == 0:
                        lmask[d, li, t, u] = 1.0
    a3 = np.concatenate([a, a, a], axis=2)
    return (jnp.asarray(a3, _BF), jnp.asarray(lmask), jnp.asarray(rowsel))


def _hgrn_body(fwd_ref, bwd_ref, seq_ref, first_ref, last_ref,
               qf_ref, zf_ref, vf_ref, qb_ref, zb_ref, vb_ref, lbf_ref, lbb_ref, s0_ref,
               a_ref, lmask_ref, rowsel_ref,
               of_ref, ob_ref, sfin_ref,
               st_ref, xs_ref, ks_ref):
    del fwd_ref, bwd_ref, seq_ref, last_ref
    c = HGRN_CHUNK
    hd = HEAD_DIM
    tb = qf_ref.shape[0]
    nh = qf_ref.shape[1] // hd
    nchunk = tb // c
    nl = lmask_ref.shape[1]
    scale = hd ** -0.5
    i = pl.program_id(1)

    @pl.when(first_ref[i] == 1)
    def _():
        for d in range(2):
            for hh in range(nh):
                st_ref[d, hh] = s0_ref[d, hh].T

    dirs = ((qf_ref, zf_ref, vf_ref, lbf_ref, of_ref), (qb_ref, zb_ref, vb_ref, lbb_ref, ob_ref))
    chains = [(d, hh) for d in range(2) for hh in range(nh)]

    def rows_of(ci, d):
        r0 = (ci if d == 0 else nchunk - 1 - ci) * c
        return slice(r0, r0 + c)

    def operands(ci, d, hh):
        lanes = slice(hh * hd, (hh + 1) * hd)
        x_ref = xs_ref.at[(ci % 2) * 2 + d]
        q = dirs[d][0][rows_of(ci, d), lanes]
        v = dirs[d][2][rows_of(ci, d), lanes]
        k = ks_ref[(ci % 2) * 2 + d, :, lanes]
        b = x_ref[0:c, lanes]
        last_row = c - 1 if d == 0 else 0
        b_end = x_ref[last_row:last_row + 1, lanes]
        return lanes, x_ref, q, v, k, b, b_end

    def state_free(ci):
        for d in range(2):
            lb = dirs[d][3][...]
            f = lb + (1.0 - lb) * _sigmoid(dirs[d][1][rows_of(ci, d), :])
            ks_ref[(ci % 2) * 2 + d] = 1.0 - f
            g1, g2, g3 = _split3(jnp.log(f) * LOG2_E)
            xs_ref[(ci % 2) * 2 + d] = _dot(a_ref[d], jnp.concatenate([g1, g2, g3], axis=0))
        qe, zls, ke, att = {}, {}, {}, {}
        for d, hh in chains:
            lanes, x_ref, q, v, k, b, b_end = operands(ci, d, hh)
            qe[d, hh] = (q * jnp.exp2(b)).astype(_BF)
            zls[d, hh] = [(jnp.where(rowsel_ref[d, li] > 0.5, q, k)
                           * jnp.exp2(x_ref[(1 + li) * c:(2 + li) * c, lanes])).astype(_BF) for li in range(nl)]
            ke[d, hh] = (k * jnp.exp2(b_end - b)).astype(_BF)
        grams = {ch: [_dot_nt(z, z) for z in zls[ch]] for ch in chains}
        for d, hh in chains:
            acc = grams[d, hh][0] * lmask_ref[d, 0]
            for li in range(1, nl):
                acc = acc + grams[d, hh][li] * lmask_ref[d, li]
            att[d, hh] = acc.astype(_BF)
        return qe, ke, att

    def state_tail(ci, qe, ke, att):
        o_inter = {(d, hh): _dot_nt(qe[d, hh], st_ref[d, hh].astype(_BF)) for d, hh in chains}
        for d, hh in chains:
            lanes, x_ref, q, v, k, b, b_end = operands(ci, d, hh)
            vb16 = v.astype(_BF)
            o = o_inter[d, hh] + _dot(att[d, hh], vb16) + jnp.sum(q * k, axis=-1, keepdims=True) * v
            dirs[d][4][rows_of(ci, d), lanes] = o * scale
            st_ref[d, hh] = st_ref[d, hh] * jnp.exp2(b_end) + _dot_tn(vb16, ke[d, hh])

    pending = state_free(0)
    for ci in range(nchunk):
        upcoming = state_free(ci + 1) if ci + 1 < nchunk else None
        state_tail(ci, *pending)
        pending = upcoming

    for d in range(2):
        for hh in range(nh):
            sfin_ref[d, hh] = st_ref[d, hh].T


def hgrn_scan(p, lb_f, lb_b, s0_all, tables, consts):
    ntok = p.shape[0]
    n_seq, _, heads = s0_all.shape[:3]
    hd = HEAD_DIM
    nh = min(HGRN_HEADS_PER_STEP, heads)
    tb = HGRN_TBLOCK
    n_items = tables[0].shape[0]
    a3, lmask, rowsel = consts
    assert heads % nh == 0
    hblocks = heads // nh

    def col(base, use_bwd):
        def imap(h, i, fwd, bwd, seq, first, last):
            return ((bwd if use_bwd else fwd)[i], base * hblocks + h)
        return pl.BlockSpec((tb, nh * hd), imap)

    def full(arr):
        nd = arr.ndim
        return pl.BlockSpec(arr.shape, lambda h, i, *_: (0,) * nd)

    state_spec = pl.BlockSpec((None, 2, nh, hd, hd),
                              lambda h, i, fwd, bwd, seq, first, last: (seq[i], 0, h, 0, 0))
    lb_spec = pl.BlockSpec((1, nh * hd), lambda h, i, *_: (0, h))
    grid_spec = pltpu.PrefetchScalarGridSpec(
        num_scalar_prefetch=5,
        grid=(hblocks, n_items),
        in_specs=[col(0, False), col(1, False), col(3, False),
                  col(0, True), col(2, True), col(3, True),
                  lb_spec, lb_spec, state_spec,
                  full(a3), full(lmask), full(rowsel)],
        out_specs=[
            pl.BlockSpec((tb, nh * hd), lambda h, i, fwd, bwd, seq, first, last: (fwd[i], h)),
            pl.BlockSpec((tb, nh * hd), lambda h, i, fwd, bwd, seq, first, last: (bwd[i], h)),
            state_spec,
        ],
        scratch_shapes=[
            pltpu.VMEM((2, nh, hd, hd), _F32),
            pltpu.VMEM((4, a3.shape[1], nh * hd), _F32),
            pltpu.VMEM((4, HGRN_CHUNK, nh * hd), _F32),
        ],
    )
    return pl.pallas_call(
        _hgrn_body,
        grid_spec=grid_spec,
        out_shape=[
            jax.ShapeDtypeStruct((ntok, heads * hd), _F32),
            jax.ShapeDtypeStruct((ntok, heads * hd), _F32),
            jax.ShapeDtypeStruct((n_seq, 2, heads, hd, hd), _F32),
        ],
        compiler_params=_cparams(("arbitrary", "arbitrary")),
        name="hgrn",
    )(*tables, p, p, p, p, p, p, lb_f, lb_b, s0_all, a3, lmask, rowsel)


def _hgrn_tables(batch, seq, dec_batch, dec_seq):
    tb = HGRN_TBLOCK
    assert seq % tb == 0 and dec_seq % tb == 0
    fwd, bwd, sq, first, last = [], [], [], [], []
    base = 0
    for s_idx, (nb, n) in enumerate([(seq // tb, batch), (dec_seq // tb, dec_batch)]):
        for s in range(n):
            for j in range(nb):
                fwd.append(base + j)
                bwd.append(base + nb - 1 - j)
                sq.append(s if s_idx == 0 else batch + s)
                first.append(int(j == 0))
                last.append(int(j == nb - 1))
            base += nb
    return tuple(jnp.asarray(np.array(t, np.int32)) for t in (fwd, bwd, sq, first, last))


def _pool_constants(tm, seg_lens):
    nw = len(POOL_WINDOWS)
    pw = np.zeros((len(seg_lens), nw, tm, tm), np.float32)
    cnt = np.zeros((len(seg_lens), nw, tm, 1), np.float32)
    for si, seg in enumerate(seg_lens):
        for wi, w in enumerate(POOL_WINDOWS):
            for t in range(tm):
                s0, ps = (t // seg) * seg, t % seg
                lo, hi = max(ps - w // 2, 0), min(ps + w // 2, seg)
                pw[si, wi, t, s0 + lo:s0 + hi] = 1.0
                cnt[si, wi, t, 0] = hi - lo
    pw3 = np.concatenate([pw, pw, pw], axis=3)
    return jnp.asarray(pw3, _BF), cnt


def _mix_body(x_ref, of_ref, ob_ref, g_ref, u_ref, mod_ref, gw_ref, pw_ref, cnt_ref, wpool_ref,
              pscale_ref, wout_ref, nw2_ref, x1_ref, h2_ref, wout_bf):
    @pl.when(pl.program_id(0) == 0)
    def _():
        wout_bf[...] = wout_ref[...].astype(_BF)

    d_a = of_ref.shape[1]
    heads = d_a // HEAD_DIM
    ngroups, gd = wpool_ref.shape[0], wpool_ref.shape[1]

    o = of_ref[...] + ob_ref[...]
    g = g_ref[...]
    gate = g * _sigmoid(g)
    pieces = []
    for h in range(heads):
        sl = slice(h * HEAD_DIM, (h + 1) * HEAD_DIM)
        pieces.append((_rmsnorm_rows(o[:, sl], gw_ref[:, sl]) * gate[:, sl]).astype(_BF))

    u = u_ref[...]
    u1, u2, u3 = _split3(u)
    us = jnp.concatenate([u1, u2, u3], axis=0)
    for gi in range(ngroups):
        sl = slice(gi * gd, (gi + 1) * gd)
        y = _dot(pw_ref[gi], us[:, sl]) / cnt_ref[gi] - u[:, sl]
        pieces.append((_dot(y.astype(_BF), wpool_ref[gi]) * pscale_ref[:, sl]).astype(_BF))

    cat = jnp.concatenate(pieces, axis=-1)
    x1 = x_ref[...] + mod_ref[2:3, :] * _dot(cat, wout_bf[...])
    x1_ref[...] = x1
    y2 = _rmsnorm_rows(x1, nw2_ref[...])
    h2_ref[...] = (y2 * (1.0 + mod_ref[4:5, :]) + mod_ref[3:4, :]).astype(_BF)


def mixer_epilogue(x, p, o_f, o_b, mods_l, gnorm_w, pool_consts, w_pool_bf, pool_scale, w_out_bf, nw2,
                   l, n_ctx, dec_seq):
    ntok, d = x.shape
    d_a = o_f.shape[1]
    d_b = p.shape[1] - N_A_PROJ * d_a
    assert d_a == d_b
    tm = TM_MIXER
    pw3, cnt = pool_consts
    cnt = jnp.broadcast_to(jnp.asarray(cnt), cnt.shape[:3] + (w_pool_bf.shape[2],))
    row = functools.partial(_mod_row, tm=tm, n_ctx=n_ctx, dec_seq=dec_seq)

    def is_sample(i):
        return jnp.where(i * tm < n_ctx, 0, 1)

    return pl.pallas_call(
        _mix_body,
        grid=(ntok // tm,),
        in_specs=[
            pl.BlockSpec((tm, d), lambda i: (i, 0)),
            pl.BlockSpec((tm, d_a), lambda i: (i, 0)),
            pl.BlockSpec((tm, d_a), lambda i: (i, 0)),
            pl.BlockSpec((tm, d_a), lambda i: (i, 4)),
            pl.BlockSpec((tm, d_b), lambda i: (i, 5)),
            pl.BlockSpec((None, N_MOD, d), lambda i: (row(i), 0, 0)),
            pl.BlockSpec((1, d_a), lambda i: (0, 0)),
            pl.BlockSpec((None,) + pw3.shape[1:], lambda i: (is_sample(i), 0, 0, 0)),
            pl.BlockSpec((None,) + cnt.shape[1:], lambda i: (is_sample(i), 0, 0, 0)),
            _resident((None,) + w_pool_bf.shape[1:], lambda i: (l, 0, 0, 0)),
            pl.BlockSpec((1, d_b), lambda i: (0, 0)),
            _resident((None,) + w_out_bf.shape[1:], lambda i: (l, 0, 0)),
            pl.BlockSpec((1, d), lambda i: (0, 0)),
        ],
        out_specs=[pl.BlockSpec((tm, d), lambda i: (i, 0)), pl.BlockSpec((tm, d), lambda i: (i, 0))],
        out_shape=[jax.ShapeDtypeStruct((ntok, d), _F32), jax.ShapeDtypeStruct((ntok, d), _BF)],
        scratch_shapes=[pltpu.VMEM(w_out_bf.shape[1:], _BF)],
        compiler_params=_cparams(("arbitrary",)),
        name="mixer",
    )(x, o_f, o_b, p, p, mods_l, gnorm_w, pw3, cnt, w_pool_bf, pool_scale, w_out_bf, nw2)


_CAND_HALF = 8


def _cand_flat_index(topk, lanes):
    assert topk == 2 * _CAND_HALF
    flat = list(range(topk))
    for a in range(1, _CAND_HALF):
        flat += [a * topk + b for b in range(_CAND_HALF)]
    flat += [a * topk for a in range(_CAND_HALF, topk)]
    return jnp.asarray(np.broadcast_to(np.array(flat, np.float32)[:, None], (len(flat), lanes)))


def _route_body(h_ref, wq_ref, keys_ref, flat_ref, gate_ref, ia_ref, ib_ref,
                q_scr, sv_scr, si_scr, ts_scr, te_scr, g_scr, a_scr, b_scr, wq_bf):
    @pl.when(pl.program_id(0) == 0)
    def _():
        wq_bf[...] = wq_ref[...].astype(_BF)

    tr = h_ref.shape[0]
    nk, half = keys_ref.shape[1], keys_ref.shape[2]
    heads = keys_ref.shape[0] // 2
    topk = PEER_TOPK
    hk = _CAND_HALF
    neg = -jnp.inf

    q = _dot(h_ref[...], wq_bf[...]).astype(_BF)
    for hp in range(2 * heads):
        q_scr[hp] = q[:, hp * half:(hp + 1) * half]

    iota_lo = lax.broadcasted_iota(jnp.int32, (nk // 2, tr), 0).astype(_F32)
    iota_hi = iota_lo + float(nk // 2)
    flat = flat_ref[...]
    ncand = float(topk * topk)

    def head(h, carry):
        for p in range(2):
            s = _dot_nt(keys_ref[2 * h + p], q_scr[2 * h + p])
            sa, sb = s[:nk // 2], s[nk // 2:]
            swap = sb > sa
            top, bot = jnp.maximum(sa, sb), jnp.minimum(sa, sb)
            top_i = jnp.where(swap, iota_hi, iota_lo)
            bot_i = jnp.where(swap, iota_lo, iota_hi)
            for r in range(topk):
                m = jnp.max(top, axis=0, keepdims=True)
                i = jnp.min(jnp.where(top == m, top_i, float(nk)), axis=0, keepdims=True)
                taken = top_i == i
                top = jnp.where(taken, bot, top)
                top_i = jnp.where(taken, bot_i, top_i)
                bot = jnp.where(taken, neg, bot)
                sv_scr[p, r:r + 1, :] = m
                si_scr[p, r:r + 1, :] = i
        s1, s2 = sv_scr[0], sv_scr[1]
        i1, i2 = si_scr[0] * float(nk), si_scr[1]
        cand = [s1[0:1] + s2] + [s1[a:a + 1] + s2[0:hk] for a in range(1, hk)] + [s1[hk:] + s2[0:1]]
        cidx = [i1[0:1] + i2] + [i1[a:a + 1] + i2[0:hk] for a in range(1, hk)] + [i1[hk:] + i2[0:1]]
        cand = jnp.concatenate(cand, axis=0)
        ckey = flat * float(nk * nk) + jnp.concatenate(cidx, axis=0)
        for r in range(topk):
            m = jnp.max(cand, axis=0, keepdims=True)
            kmin = jnp.min(jnp.where(cand == m, ckey, ncand * nk * nk), axis=0, keepdims=True)
            te_scr[r:r + 1, :] = kmin
            ts_scr[r:r + 1, :] = m
            cand = jnp.where(ckey == kmin, neg, cand)
        top_s = ts_scr[...]
        e = te_scr[...]
        e = e - jnp.floor(e * (1.0 / (nk * nk))) * float(nk * nk)
        ex = jnp.exp(top_s - jnp.max(top_s, axis=0, keepdims=True))
        gate = ex / jnp.sum(ex, axis=0, keepdims=True)
        rows = pl.ds(pl.multiple_of(h * topk, topk), topk)
        g_scr[rows, :] = gate
        first_key = jnp.floor(e * (1.0 / nk))
        a_scr[rows, :] = first_key
        b_scr[rows, :] = e - first_key * float(nk)
        return carry

    lax.fori_loop(0, heads, head, 0)
    gate_ref[...] = g_scr[...].T
    ia_ref[...] = a_scr[...].T
    ib_ref[...] = b_scr[...].T


def peer_route(h2, w_query_bf, keys_bf, l):
    ntok, d = h2.shape
    hq = w_query_bf.shape[2]
    nhp, nk, half = keys_bf.shape[1:]
    tr = min(TM_ROUTE, ntok)
    topk = PEER_TOPK
    npair = (nhp // 2) * topk
    assert npair == nk and ntok % tr == 0
    flat = _cand_flat_index(topk, tr)
    out = jax.ShapeDtypeStruct((ntok, npair), _F32)
    spec = pl.BlockSpec((tr, npair), lambda i: (i, 0))
    return pl.pallas_call(
        _route_body,
        grid=(ntok // tr,),
        in_specs=[
            pl.BlockSpec((tr, d), lambda i: (i, 0)),
            _resident((None, d, hq), lambda i: (l, 0, 0)),
            _resident((None, nhp, nk, half), lambda i: (l, 0, 0, 0)),
            pl.BlockSpec(flat.shape, lambda i: (0, 0)),
        ],
        out_specs=[spec, spec, spec],
        out_shape=[out, out, out],
        scratch_shapes=[
            pltpu.VMEM((nhp, tr, half), _BF),
            pltpu.VMEM((2, topk, tr), _F32),
            pltpu.VMEM((2, topk, tr), _F32),
            pltpu.VMEM((topk, tr), _F32),
            pltpu.VMEM((topk, tr), _F32),
            pltpu.VMEM((npair, tr), _F32),
            pltpu.VMEM((npair, tr), _F32),
            pltpu.VMEM((npair, tr), _F32),
            pltpu.VMEM((d, hq), _BF),
        ],
        compiler_params=_cparams(("arbitrary",)),
        name="peer_route",
    )(h2, w_query_bf, keys_bf, flat)


def _wbuild_body(g_ref, a_ref, b_ref, w_ref):
    tw, npair = g_ref.shape
    nk = w_ref.shape[1]
    iota = lax.broadcasted_iota(jnp.int32, (nk, npair), 0).astype(_F32)

    def one_group(gi):
        t0 = pl.multiple_of(gi * SUBLANES, SUBLANES)
        g8 = g_ref[pl.ds(t0, SUBLANES), :]
        a8 = a_ref[pl.ds(t0, SUBLANES), :]
        b8 = b_ref[pl.ds(t0, SUBLANES), :]
        base = pl.multiple_of(gi * (SUBLANES * nk), SUBLANES * nk)
        ats, bts = [], []
        for k in range(SUBLANES):
            g = g8[k:k + 1]
            g_hi = g.astype(_BF).astype(_F32)
            g_lo = g - g_hi
            hit_a = iota == a8[k:k + 1]
            hit_b = iota == b8[k:k + 1]
            ats.append(jnp.concatenate([jnp.where(hit_a, g_hi, 0.0), jnp.where(hit_a, g_lo, 0.0)],
                                       axis=1).astype(_BF))
            bt1 = jnp.where(hit_b, 1.0, 0.0).astype(_BF)
            bts.append(jnp.concatenate([bt1, bt1], axis=1))
        w_ts = [_dot_nt(at, bt) for at, bt in zip(ats, bts)]
        for k in range(SUBLANES):
            w_ref[pl.ds(base + k, nk, stride=SUBLANES), :] = w_ts[k]

    def groups(it, carry):
        for u in range(WBUILD_GROUPS_PER_ITER):
            one_group(it * WBUILD_GROUPS_PER_ITER + u)
        return carry

    lax.fori_loop(0, tw // (SUBLANES * WBUILD_GROUPS_PER_ITER), groups, 0)


def peer_weights(gate, ia, ib, nk):
    ntok, npair = gate.shape
    tw = TM_WBUILD
    spec = pl.BlockSpec((tw, npair), lambda i: (i, 0))
    w = pl.pallas_call(
        _wbuild_body,
        grid=(ntok // tw,),
        in_specs=[spec, spec, spec],
        out_specs=pl.BlockSpec((tw * nk, nk), lambda i: (i, 0)),
        out_shape=jax.ShapeDtypeStruct((ntok * nk, nk), _F32),
        compiler_params=_cparams(("arbitrary",)),
        name="peer_weights",
    )(gate, ia, ib)
    return w.reshape(ntok // SUBLANES, nk, SUBLANES, nk)


def _ffn_body(h_ref, u_ref, v_ref, w_ref, x_ref, mod_ref, fw_ref, o_ref, *, final):
    j = pl.program_id(1)
    tm = h_ref.shape[0]
    nsub, nk = w_ref.shape[1], w_ref.shape[3]

    @pl.when(j == 0)
    def _():
        o_ref[...] = jnp.zeros_like(o_ref)

    s = _dot_nt(h_ref[...], u_ref[...].astype(_BF))
    act = 0.5 * s * (1.0 + lax.erf(s * (2.0 ** -0.5)))
    w = jnp.concatenate([w_ref[:, jl].reshape(tm, nk) for jl in range(nsub)], axis=1)
    o_ref[...] += _dot((act * w).astype(_BF), v_ref[...].astype(_BF))

    @pl.when(j == pl.num_programs(1) - 1)
    def _():
        x2 = x_ref[...] + mod_ref[5:6, :] * o_ref[...]
        if final:
            x2 = _rmsnorm_rows(x2, fw_ref[...])
        o_ref[...] = x2


def peer_ffn(h2, u_bf, v_bf, w, x1, mods_l, final_w, l, n_ctx, dec_seq, final):
    ntok, d = h2.shape
    ne = u_bf.shape[1]
    nk = w.shape[3]
    tm = min(TM_FFN, n_ctx, dec_seq)
    te = FFN_SUB * nk
    assert ne % te == 0 and ntok % tm == 0 and n_ctx % tm == 0 and dec_seq % tm == 0
    row = functools.partial(_mod_row, tm=tm, n_ctx=n_ctx, dec_seq=dec_seq)
    once = pl.Buffered(1)
    return pl.pallas_call(
        functools.partial(_ffn_body, final=final),
        grid=(ntok // tm, ne // te),
        in_specs=[
            pl.BlockSpec((tm, d), lambda i, j: (i, 0), pipeline_mode=once),
            pl.BlockSpec((None, te, d), lambda i, j: (l, j, 0)),
            pl.BlockSpec((None, te, d), lambda i, j: (l, j, 0)),
            pl.BlockSpec((tm // SUBLANES, FFN_SUB, SUBLANES, nk), lambda i, j: (i, j, 0, 0)),
            pl.BlockSpec((tm, d), lambda i, j: (i, 0), pipeline_mode=once),
            pl.BlockSpec((None, N_MOD, d), lambda i, j: (row(i), 0, 0)),
            pl.BlockSpec((1, d), lambda i, j: (0, 0)),
        ],
        out_specs=pl.BlockSpec((tm, d), lambda i, j: (i, 0), pipeline_mode=once),
        out_shape=jax.ShapeDtypeStruct((ntok, d), _F32),
        compiler_params=_cparams(("arbitrary", "arbitrary")),
        name="peer_ffn",
    )(h2, u_bf, v_bf, w, x1, mods_l, final_w)


def kernel(x_prompt, x_sample, state_hgrn, c, c_ctx, w_ada, b_ada, norm_w, w_in, lower_bounds, hgrn_norm_w,
           w_pool, pool_scale, w_out, w_query, sub_keys, expert_u, expert_v, final_norm_w):
    batch, seq, d = x_prompt.shape
    dec_batch, dec_seq, _ = x_sample.shape
    depth = w_ada.shape[0]
    d_a = lower_bounds.shape[2]
    heads = d_a // HEAD_DIM
    n_ctx = batch * seq
    nk = sub_keys.shape[3]
    assert dec_batch + 1 <= SUBLANES and dec_seq % GRID_W == 0
    assert seq == TM_MIXER and (nk & (nk - 1)) == 0

    lb_soft = jax.nn.softmax(lower_bounds.astype(_F32), axis=1)
    lb = jnp.cumsum(lb_soft, axis=1) - lb_soft[:, :1]

    cv = jnp.zeros((SUBLANES, d), _F32).at[0].set(c_ctx).at[1:1 + dec_batch].set(c)
    mods = adaln_all(cv, w_ada, b_ada)[:, :1 + dec_batch].reshape(depth, 1 + dec_batch, N_MOD, d)

    x = jnp.concatenate([x_prompt.reshape(n_ctx, d), x_sample.reshape(dec_batch * dec_seq, d)], axis=0)
    tables = _hgrn_tables(batch, seq, dec_batch, dec_seq)
    hconsts = _hgrn_constants(HGRN_CHUNK)
    pool_consts = _pool_constants(TM_MIXER, (seq, GRID_W))
    zero_state = jnp.zeros((batch, 2, heads, HEAD_DIM, HEAD_DIM), _F32)

    w_in_bf = w_in
    w_pool_bf = w_pool.astype(_BF)
    w_out_bf, w_query_bf = w_out, w_query
    keys_bf = sub_keys.reshape(depth, -1, nk, sub_keys.shape[4]).astype(_BF)
    u_bf, v_bf = expert_u, expert_v

    states = []
    for l in range(depth):
        p = in_projection(x, mods[l], norm_w[l, 0:1], w_in_bf, l, n_ctx, dec_seq)
        s0_all = jnp.concatenate([zero_state, state_hgrn[:, l].astype(_F32)], axis=0)
        o_f, o_b, s_fin = hgrn_scan(p, lb[0, l][None], lb[1, l][None], s0_all, tables, hconsts)
        states.append(s_fin[:batch])
        x1, h2 = mixer_epilogue(x, p, o_f, o_b, mods[l], hgrn_norm_w[l][None], pool_consts,
                                w_pool_bf, pool_scale[l][None], w_out_bf, norm_w[l, 1:2], l, n_ctx, dec_seq)
        gate, ia, ib = peer_route(h2, w_query_bf, keys_bf, l)
        w = peer_weights(gate, ia, ib, nk)
        x = peer_ffn(h2, u_bf, v_bf, w, x1, mods[l], final_norm_w[None], l, n_ctx, dec_seq,
                     final=(l == depth - 1))

    y_prompt = x[:n_ctx].reshape(batch, seq, d)
    y_sample = x[n_ctx:].reshape(dec_batch, dec_seq, d)
    state_new = jnp.stack(states, axis=1).astype(x_prompt.dtype)
    return (y_prompt, y_sample, state_new)
```
